```python
import math
import jax, jax.numpy as jnp
from jax import lax
import numpy as np

D_MODEL = 2048
BATCH = 8
SEQ = 2048
DEPTH = 1
DEC_BATCH = 128
DEC_SEQ = 8
PAST_LEN = 16384
PAGE_SIZE = 128

MIX_WIDTH = D_MODEL
DIFF_WIDTH = MIX_WIDTH // 2
MLA_WIDTH = MIX_WIDTH - DIFF_WIDTH
DIFF_HEADS = 8
DIFF_HEAD_DIM = DIFF_WIDTH // DIFF_HEADS // 2
DIFF_KV_HEADS = 4
DIFF_GROUP = DIFF_HEADS // DIFF_KV_HEADS
MLA_HEADS = 8
MLA_V_DIM = MLA_WIDTH // MLA_HEADS
MLA_NOPE_DIM = 128
MLA_ROPE_DIM = 64
Q_LORA_RANK = 512
KV_LORA_RANK = 256
ROPE_THETA = 10000.0
REL_BUCKETS = 32
REL_MAX_EXACT = REL_BUCKETS // 2
REL_MAX_DIST = 128
Q_BLOCK = 128
EPS = 1e-6
NEG_INF = -1e30

COL_DIFF_Q = DIFF_HEADS * 2 * DIFF_HEAD_DIM
COL_DIFF_K = DIFF_KV_HEADS * 2 * DIFF_HEAD_DIM
COL_DIFF_V = DIFF_KV_HEADS * 2 * DIFF_HEAD_DIM
COL_GATE_A = DIFF_WIDTH
COL_Q_A = Q_LORA_RANK
COL_KV_A = KV_LORA_RANK + MLA_ROPE_DIM
COL_GATE_B = MLA_WIDTH
IN_WIDTH = COL_DIFF_Q + COL_DIFF_K + COL_DIFF_V + COL_GATE_A + COL_Q_A + COL_KV_A + COL_GATE_B

kernel_name = "hymba_diffattn_mla_adaln_step"


def rmsnorm(x, g):
    xf = x.astype(jnp.float32)
    y = xf * lax.rsqrt(jnp.mean(xf * xf, axis=-1, keepdims=True) + EPS)
    return (y * g.astype(jnp.float32)).astype(x.dtype)


def rope(x, pos):
    r = x.shape[-1]
    freqs = ROPE_THETA ** (-jnp.arange(0, r, 2, dtype=jnp.float32) / r)
    ang = pos.astype(jnp.float32)[:, None] * freqs[None, :]
    ang = ang.reshape((pos.shape[0],) + (1,) * (x.ndim - 3) + (r // 2,))
    cos, sin = jnp.cos(ang), jnp.sin(ang)
    xf = x.astype(jnp.float32)
    x1, x2 = xf[..., : r // 2], xf[..., r // 2:]
    return jnp.concatenate([x1 * cos - x2 * sin, x1 * sin + x2 * cos], axis=-1).astype(x.dtype)


def rel_bias(table, q_pos, k_pos):
    n = jnp.maximum(q_pos[:, None] - k_pos[None, :], 0)
    nf = jnp.maximum(n, 1).astype(jnp.float32)
    large = REL_MAX_EXACT + (jnp.log(nf / REL_MAX_EXACT) / math.log(REL_MAX_DIST / REL_MAX_EXACT)
                             * (REL_BUCKETS - REL_MAX_EXACT)).astype(jnp.int32)
    large = jnp.minimum(large, REL_BUCKETS - 1)
    bucket = jnp.where(n < REL_MAX_EXACT, n, large)
    b = table.astype(jnp.float32)[bucket]
    tq, tk = bucket.shape
    return jnp.transpose(b, (2, 0, 1)).reshape(DIFF_KV_HEADS, DIFF_GROUP, tq, tk)


def diff_attend(q, k, v, q_pos, k_pos, table, lam):
    scale = DIFF_HEAD_DIM ** -0.5
    k = k.reshape(k.shape[:-1] + (2, DIFF_HEAD_DIM))
    s = jnp.einsum('bqgrmd,bkgmd->mbgrqk', q, k).astype(jnp.float32) * scale
    mask = k_pos[None, :] <= q_pos[:, None]
    s = jnp.where(mask, s + rel_bias(table, q_pos, k_pos), NEG_INF)
    p = jax.nn.softmax(s, axis=-1)
    a = p[0] - lam * p[1]
    return jnp.einsum('bgrqk,bkgd->bqgrd', a.astype(v.dtype), v)


def mla_attend(q_lat, q_pe, ckv, kpe, q_pos, k_pos):
    scale = (MLA_NOPE_DIM + MLA_ROPE_DIM) ** -0.5
    s = (jnp.einsum('bqhc,bkc->bhqk', q_lat, ckv)
         + jnp.einsum('bqhr,bkr->bhqk', q_pe, kpe)).astype(jnp.float32) * scale
    mask = k_pos[None, :] <= q_pos[:, None]
    s = jnp.where(mask, s, NEG_INF)
    p = jax.nn.softmax(s, axis=-1)
    return jnp.einsum('bhqk,bkc->bqhc', p.astype(ckv.dtype), ckv)


def modulate(x, c, w_mod, b_mod, gain):
    mod = jax.nn.silu(c) @ w_mod + b_mod
    shift, scale, gate = jnp.split(mod, 3, axis=-1)
    h = rmsnorm(x, gain) * (1 + scale[:, None, :]) + shift[:, None, :]
    return h, gate[:, None, :]


def project(h, pos, w_in, q_a_norm, kv_a_norm, w_qb, w_uk):
    b, t, _ = h.shape
    z = h @ w_in
    cuts = [COL_DIFF_Q, COL_DIFF_K, COL_DIFF_V, COL_GATE_A, COL_Q_A, COL_KV_A]
    idx = [sum(cuts[:i + 1]) for i in range(len(cuts))]
    q_d, k_d, v_d, g_a, q_a, kv_a, g_b = jnp.split(z, idx, axis=-1)
    q_d = q_d.reshape(b, t, DIFF_KV_HEADS, DIFF_GROUP, 2, DIFF_HEAD_DIM)
    k_d = k_d.reshape(b, t, DIFF_KV_HEADS, 2 * DIFF_HEAD_DIM)
    v_d = v_d.reshape(b, t, DIFF_KV_HEADS, 2 * DIFF_HEAD_DIM)
    q = jnp.einsum('btc,chd->bthd', rmsnorm(q_a, q_a_norm), w_qb)
    q_pe = rope(q[..., MLA_NOPE_DIM:], pos)
    q_lat = jnp.einsum('bthd,chd->bthc', q[..., :MLA_NOPE_DIM], w_uk)
    ckv = rmsnorm(kv_a[..., :KV_LORA_RANK], kv_a_norm)
    kpe = rope(kv_a[..., KV_LORA_RANK:], pos)
    return q_d, k_d, v_d, g_a, q_lat, q_pe, ckv, kpe, g_b


def merge(o_d, o_lat, g_a, g_b, subln_gain, lam_init, w_uv, w_out):
    b, t = o_d.shape[:2]
    o_d = (rmsnorm(o_d, subln_gain) * (1 - lam_init)).reshape(b, t, DIFF_WIDTH) * jax.nn.silu(g_a)
    o_m = jnp.einsum('bthc,chd->bthd', o_lat, w_uv).reshape(b, t, MLA_WIDTH) * jax.nn.silu(g_b)
    return jnp.concatenate([o_d, o_m], axis=-1) @ w_out


def setup_inputs(seed: int = 0) -> dict:
    key = jax.random.key(seed)
    ks = jax.random.split(key, 24)
    f32 = jnp.float32
    n_pages = PAST_LEN // PAGE_SIZE
    n_pool = (DEC_BATCH * n_pages * 5) // 4
    nrm = lambda k, s: jax.random.normal(k, s, f32)
    page_table = jax.random.permutation(ks[8], n_pool)[: DEC_BATCH * n_pages]
    page_table = page_table.reshape(DEC_BATCH, n_pages).astype(jnp.int32)
    return {
        "x_prompt": nrm(ks[0], (BATCH, SEQ, D_MODEL)),
        "x_sample": nrm(ks[1], (DEC_BATCH, DEC_SEQ, D_MODEL)),
        "c_prompt": nrm(ks[2], (BATCH, D_MODEL)),
        "c_sample": nrm(ks[3], (DEC_BATCH, D_MODEL)),
        "cache_diff_k": nrm(ks[4], (DEPTH, n_pool, PAGE_SIZE, DIFF_KV_HEADS, 2 * DIFF_HEAD_DIM)),
        "cache_diff_v": nrm(ks[5], (DEPTH, n_pool, PAGE_SIZE, DIFF_KV_HEADS, 2 * DIFF_HEAD_DIM)),
        "cache_mla_ckv": nrm(ks[6], (DEPTH, n_pool, PAGE_SIZE, KV_LORA_RANK)),
        "cache_mla_kpe": nrm(ks[7], (DEPTH, n_pool, PAGE_SIZE, MLA_ROPE_DIM)),
        "page_table": page_table,
        "w_mod": nrm(ks[9], (DEPTH, D_MODEL, 3 * D_MODEL)) * (0.5 * D_MODEL ** -0.5),
        "b_mod": nrm(ks[10], (DEPTH, 3 * D_MODEL)) * 0.01,
        "norm_gain": 1.0 + 0.02 * nrm(ks[11], (DEPTH, D_MODEL)),
        "w_in": nrm(ks[12], (DEPTH, D_MODEL, IN_WIDTH)) * D_MODEL ** -0.5,
        "lambda_params": 0.1 * nrm(ks[13], (DEPTH, 4, DIFF_HEAD_DIM)),
        "subln_gain": 1.0 + 0.02 * nrm(ks[14], (DEPTH, 2 * DIFF_HEAD_DIM)),
        "q_a_norm": 1.0 + 0.02 * nrm(ks[15], (DEPTH, Q_LORA_RANK)),
        "kv_a_norm": 1.0 + 0.02 * nrm(ks[16], (DEPTH, KV_LORA_RANK)),
        "w_qb": nrm(ks[17], (DEPTH, Q_LORA_RANK, MLA_HEADS, MLA_NOPE_DIM + MLA_ROPE_DIM)) * Q_LORA_RANK ** -0.5,
        "w_kvb": nrm(ks[18], (DEPTH, KV_LORA_RANK, MLA_HEADS, MLA_NOPE_DIM + MLA_V_DIM)) * KV_LORA_RANK ** -0.5,
        "w_out": nrm(ks[19], (DEPTH, MIX_WIDTH, D_MODEL)) * MIX_WIDTH ** -0.5,
        "rel_bias": 0.5 * nrm(ks[20], (REL_BUCKETS, DIFF_HEADS)),
        "final_norm": 1.0 + 0.02 * nrm(ks[21], (D_MODEL,)),
    }


def reference(x_prompt, x_sample, c_prompt, c_sample, cache_diff_k, cache_diff_v, cache_mla_ckv,
              cache_mla_kpe, page_table, w_mod, b_mod, norm_gain, w_in, lambda_params, subln_gain,
              q_a_norm, kv_a_norm, w_qb, w_kvb, w_out, rel_bias, final_norm):
    n_blocks = SEQ // Q_BLOCK
    pos_p = jnp.arange(SEQ, dtype=jnp.int32)
    pos_s = PAST_LEN + jnp.arange(DEC_SEQ, dtype=jnp.int32)
    k_pos_s = jnp.arange(PAST_LEN + DEC_SEQ, dtype=jnp.int32)
    x_p, x_s = x_prompt, x_sample
    nk_p, nv_p, nc_p, nr_p = [], [], [], []
    nk_s, nv_s, nc_s, nr_s = [], [], [], []

    def to_blocks(t):
        return jnp.moveaxis(t.reshape((t.shape[0], n_blocks, Q_BLOCK) + t.shape[2:]), 1, 0)

    def from_blocks(t):
        t = jnp.moveaxis(t, 0, 1)
        return t.reshape((t.shape[0], SEQ) + t.shape[3:])

    for l in range(DEPTH):
        lam_init = 0.8 - 0.6 * math.exp(-0.3 * l)
        lp = lambda_params[l].astype(jnp.float32)
        lam = jnp.exp(jnp.sum(lp[0] * lp[1])) - jnp.exp(jnp.sum(lp[2] * lp[3])) + lam_init
        w_uk = w_kvb[l][..., :MLA_NOPE_DIM]
        w_uv = w_kvb[l][..., MLA_NOPE_DIM:]

        h, gate = modulate(x_p, c_prompt, w_mod[l], b_mod[l], norm_gain[l])
        q_d, k_d, v_d, g_a, q_lat, q_pe, ckv, kpe, g_b = project(
            h, pos_p, w_in[l], q_a_norm[l], kv_a_norm[l], w_qb[l], w_uk)

        def prompt_block(args, k_d=k_d, v_d=v_d, ckv=ckv, kpe=kpe, lam=lam):
            qd, ql, qp, qpos = args
            od = diff_attend(qd, k_d, v_d, qpos, pos_p, rel_bias, lam)
            om = mla_attend(ql, qp, ckv, kpe, qpos, pos_p)
            return od, om

        od, om = lax.map(prompt_block, (to_blocks(q_d), to_blocks(q_lat), to_blocks(q_pe),
                                        pos_p.reshape(n_blocks, Q_BLOCK)))
        y = merge(from_blocks(od), from_blocks(om), g_a, g_b, subln_gain[l], lam_init, w_uv, w_out[l])
        x_p = x_p + gate * y
        nk_p.append(k_d); nv_p.append(v_d); nc_p.append(ckv); nr_p.append(kpe)

        h, gate = modulate(x_s, c_sample, w_mod[l], b_mod[l], norm_gain[l])
        q_d, k_d, v_d, g_a, q_lat, q_pe, ckv, kpe, g_b = project(
            h, pos_s, w_in[l], q_a_norm[l], kv_a_norm[l], w_qb[l], w_uk)
        ck, cv, cc, cr = cache_diff_k[l], cache_diff_v[l], cache_mla_ckv[l], cache_mla_kpe[l]

        def sample_seq(args, ck=ck, cv=cv, cc=cc, cr=cr, lam=lam):
            pt, qd, kd, vd, ql, qp, cn, kp = args

            def past(cache, new):
                rows = cache[pt]
                rows = rows.reshape((pt.shape[0] * PAGE_SIZE,) + cache.shape[2:])
                return jnp.concatenate([rows, new], axis=0)[None]

            od = diff_attend(qd[None], past(ck, kd), past(cv, vd), pos_s, k_pos_s, rel_bias, lam)
            om = mla_attend(ql[None], qp[None], past(cc, cn), past(cr, kp), pos_s, k_pos_s)
            return od[0], om[0]

        od, om = lax.map(sample_seq, (page_table, q_d, k_d, v_d, q_lat, q_pe, ckv, kpe))
        y = merge(od, om, g_a, g_b, subln_gain[l], lam_init, w_uv, w_out[l])
        x_s = x_s + gate * y
        nk_s.append(k_d); nv_s.append(v_d); nc_s.append(ckv); nr_s.append(kpe)

    y_prompt = rmsnorm(x_p, final_norm)
    y_sample = rmsnorm(x_s, final_norm)
    new_diff_k_prompt = jnp.stack(nk_p, axis=0)
    new_diff_v_prompt = jnp.stack(nv_p, axis=0)
    new_mla_ckv_prompt = jnp.stack(nc_p, axis=0)
    new_mla_kpe_prompt = jnp.stack(nr_p, axis=0)
    new_diff_k_sample = jnp.stack(nk_s, axis=0)
    new_diff_v_sample = jnp.stack(nv_s, axis=0)
    new_mla_ckv_sample = jnp.stack(nc_s, axis=0)
    new_mla_kpe_sample = jnp.stack(nr_s, axis=0)
    return (y_prompt, y_sample, new_diff_k_prompt, new_diff_v_prompt, new_mla_ckv_prompt,
            new_mla_kpe_prompt, new_diff_k_sample, new_diff_v_sample, new_mla_ckv_sample,
            new_mla_kpe_sample)
```

```python
import functools
import math

import numpy as np
import jax
import jax.numpy as jnp
from jax import lax
from jax.experimental import pallas as pl
from jax.experimental.pallas import tpu as pltpu

F32 = jnp.float32
BF16 = jnp.bfloat16

DIFF_HEADS = 8
DIFF_KV_HEADS = 4
DIFF_GROUP = DIFF_HEADS // DIFF_KV_HEADS
MLA_HEADS = 8
MLA_NOPE_DIM = 128
MLA_ROPE_DIM = 64
ROPE_THETA = 10000.0
REL_BUCKETS = 32
REL_MAX_EXACT = REL_BUCKETS // 2
REL_MAX_DIST = 128
EPS = 1e-6
NEG_INF = -1e30

LANES = 128
V7X_VMEM_BYTES = 64 * 1024 * 1024
VMEM_LIMIT = 56 * 1024 * 1024
ROW_TILE = 256
ATT_TILE = 512
MLA_PAD = 3 * LANES
SAMPLE_PAGES_PER_STEP = 16


def _cparams(sem, vmem=None):
    return pltpu.CompilerParams(dimension_semantics=sem, vmem_limit_bytes=vmem)


def _resident(shape):
    nd = len(shape)
    return pl.BlockSpec(shape, lambda *_: (0,) * nd, pipeline_mode=pl.Buffered(1))


def _rel_bucket(n):
    n = np.asarray(n, np.int64)
    nf = np.maximum(n, 1).astype(np.float32)
    large = REL_MAX_EXACT + (
        np.log(nf / np.float32(REL_MAX_EXACT)) / np.float32(math.log(REL_MAX_DIST / REL_MAX_EXACT))
        * np.float32(REL_BUCKETS - REL_MAX_EXACT)).astype(np.int32)
    large = np.minimum(large, REL_BUCKETS - 1)
    return np.where(n < REL_MAX_EXACT, n, large).astype(np.int32)


def _mod_kernel(c_ref, w_ref, b_ref, o_ref):
    c = c_ref[...]
    a = (c * jax.nn.sigmoid(c)).astype(BF16)
    o_ref[...] = jnp.dot(a, w_ref[...].astype(BF16), preferred_element_type=F32) + b_ref[...]


def _modulation(c, w_mod, b_mod):
    rows, d = c.shape
    n = w_mod.shape[1]
    tn = 1536
    return pl.pallas_call(
        _mod_kernel,
        grid=(n // tn,),
        in_specs=[pl.BlockSpec((rows, d), lambda j: (0, 0)),
                  pl.BlockSpec((d, tn), lambda j: (0, j)),
                  pl.BlockSpec((1, tn), lambda j: (0, j))],
        out_specs=pl.BlockSpec((rows, tn), lambda j: (0, j)),
        out_shape=jax.ShapeDtypeStruct((rows, n), F32),
        compiler_params=_cparams(("arbitrary",), 48 * 1024 * 1024),
        name="modulation",
    )(c, w_mod, b_mod)


def _bias_kernel(tab_ref, bp_ref, bs_ref, hs_ref, op_ref, os_ref):
    far = REL_BUCKETS - 1
    for h in range(DIFF_HEADS):
        for v in range(2):
            bkt = bp_ref[v]
            acc = jnp.zeros(bkt.shape, F32)
            for b in range(far):
                acc = jnp.where(bkt == b, tab_ref[b, h] - tab_ref[far, h], acc)
            op_ref[h, v] = acc
    head = hs_ref[...]
    for v in range(2):
        bkt = bs_ref[v]
        acc = jnp.zeros(bkt.shape, F32)
        for b in range(far):
            val = jnp.zeros(bkt.shape, F32)
            for h in range(DIFF_HEADS):
                val = jnp.where(head == h, tab_ref[b, h] - tab_ref[far, h], val)
            acc = jnp.where(bkt == b, val, acc)
        os_ref[v] = jnp.where(bkt == REL_BUCKETS, NEG_INF, acc)


def _bias_tiles(rel_bias, dec_seq):
    i = np.arange(LANES)[:, None]
    j = np.arange(LANES)[None, :]
    bp = np.stack([_rel_bucket(np.maximum(i - j, 0)), _rel_bucket(LANES + i - j)])
    t = i % dec_seq
    head = ((i // (4 * dec_seq)) * DIFF_GROUP + (i // dec_seq) % DIFF_GROUP) + 0 * j
    last = _rel_bucket(LANES + t - j)
    new = np.where(j <= t, _rel_bucket(np.maximum(t - j, 0)), REL_BUCKETS)
    bs = np.stack([last, new])
    vm = pl.BlockSpec(memory_space=pltpu.VMEM)
    return pl.pallas_call(
        _bias_kernel,
        in_specs=[pl.BlockSpec(memory_space=pltpu.SMEM), vm, vm, vm],
        out_specs=[vm, vm],
        out_shape=[jax.ShapeDtypeStruct((DIFF_HEADS, 2, LANES, LANES), F32),
                   jax.ShapeDtypeStruct((2, LANES, LANES), F32)],
        name="rel_bias_tiles",
    )(rel_bias, jnp.asarray(bp, jnp.int32), jnp.asarray(bs, jnp.int32),
      jnp.asarray(head, jnp.int32))


def _rms(x, gain):
    return x * lax.rsqrt(jnp.mean(x * x, axis=-1, keepdims=True) + EPS) * gain


def _modulated(x_ref, gain_ref, sc_ref, sh_ref):
    nb, tt, d = x_ref.shape
    h = _rms(x_ref[...], gain_ref[...]) * (1.0 + sc_ref[...]) + sh_ref[...]
    return h.reshape(nb * tt, d).astype(BF16)


def _proj_kernel(x_ref, sh_ref, sc_ref, gain_ref, w_ref, qag_ref, wqb_ref, wuk_ref, kvg_ref,
                 cq_ref, sq_ref, ck_ref, sk_ref,
                 qd_ref, kd32_ref, vd32_ref, kd16_ref, vd16_ref, qcat_ref, ckv32_ref, kpe32_ref,
                 kcat_ref, *, cols):
    nb, tt, _ = x_ref.shape
    tm = nb * tt
    h = _modulated(x_ref, gain_ref, sc_ref, sh_ref)

    def seg(name):
        a, b = cols[name]
        return jnp.dot(h, w_ref[:, a:b], preferred_element_type=F32)

    qd_ref[...] = (seg("q_d") * 0.125).astype(BF16)
    kd = seg("k_d")
    kd32_ref[...] = kd
    kd16_ref[...] = kd.astype(BF16)
    vd = seg("v_d")
    vd32_ref[...] = vd
    vd16_ref[...] = vd.astype(BF16)

    qn = _rms(seg("q_a"), qag_ref[...]).astype(BF16)
    nope_w = MLA_HEADS * MLA_NOPE_DIM
    pe_w = MLA_HEADS * LANES
    qall = jnp.dot(qn, wqb_ref[...], preferred_element_type=F32)
    pe = qall[:, nope_w:nope_w + pe_w].reshape(nb, tt, pe_w)
    pe_sw = qall[:, nope_w + pe_w:].reshape(nb, tt, pe_w)
    qpe = (pe * cq_ref[...] + pe_sw * sq_ref[...]).reshape(tm, pe_w).astype(BF16)
    lat_w = MLA_PAD - LANES
    for hh in range(MLA_HEADS):
        nope = qall[:, hh * MLA_NOPE_DIM:(hh + 1) * MLA_NOPE_DIM].astype(BF16)
        ql = jnp.dot(nope, wuk_ref[hh], preferred_element_type=F32)
        qcat_ref[:, hh * MLA_PAD:hh * MLA_PAD + lat_w] = ql.astype(BF16)
        qcat_ref[:, hh * MLA_PAD + lat_w:(hh + 1) * MLA_PAD] = qpe[:, hh * LANES:(hh + 1) * LANES]

    ckv = _rms(seg("kv"), kvg_ref[...])
    ckv32_ref[...] = ckv
    kcat_ref[:, :lat_w] = ckv.astype(BF16)
    kp = (seg("kpe").reshape(nb, tt, LANES) * ck_ref[...]
          + seg("kpe_sw").reshape(nb, tt, LANES) * sk_ref[...]).reshape(tm, LANES)
    kpe32_ref[...] = kp[:, :MLA_ROPE_DIM]
    kcat_ref[:, lat_w:] = kp.astype(BF16)


def _row_tiling(nseq, tt):
    if tt >= ROW_TILE:
        assert tt % ROW_TILE == 0
        return 1, ROW_TILE
    nb = min(ROW_TILE // tt, nseq)
    assert nseq % nb == 0
    return nb, tt


def _projection(x3, shift, scale, gain, wts, tabs):
    nseq, tt, d = x3.shape
    nb, tmt = _row_tiling(nseq, tt)
    tm = nb * tmt
    rows = nseq * tt
    nt = tt // tmt
    grid = (nseq // nb, nt)

    def rowblk(width):
        return pl.BlockSpec((tm, width), lambda i, j: (i * nt + j, 0))

    def seqblk():
        return pl.BlockSpec((nb, 1, d), lambda i, j: (i, 0, 0))

    def tabblk(width):
        return pl.BlockSpec((1, tmt, width), lambda i, j: (0, j, 0))

    diff_q = wts["cols"]["q_d"][1] - wts["cols"]["q_d"][0]
    diff_kv = wts["cols"]["k_d"][1] - wts["cols"]["k_d"][0]
    lat_w = MLA_PAD - LANES
    pe_w = MLA_HEADS * LANES
    out_shapes = [
        ((rows, diff_q), BF16), ((rows, diff_kv), F32), ((rows, diff_kv), F32),
        ((rows, diff_kv), BF16), ((rows, diff_kv), BF16), ((rows, MLA_HEADS * MLA_PAD), BF16),
        ((rows, lat_w), F32), ((rows, MLA_ROPE_DIM), F32), ((rows, MLA_PAD), BF16)]
    return pl.pallas_call(
        functools.partial(_proj_kernel, cols=wts["cols"]),
        grid=grid,
        in_specs=[pl.BlockSpec((nb, tmt, d), lambda i, j: (i, j, 0)), seqblk(), seqblk(),
                  _resident((1, d)), _resident(wts["w_in"].shape), _resident((1, wts["qa_gain"].shape[1])),
                  _resident(wts["w_qb"].shape), _resident(wts["w_uk"].shape),
                  _resident((1, lat_w)),
                  tabblk(pe_w), tabblk(pe_w), tabblk(LANES), tabblk(LANES)],
        out_specs=[rowblk(s[1]) for s, _ in out_shapes],
        out_shape=[jax.ShapeDtypeStruct(s, dt) for s, dt in out_shapes],
        compiler_params=_cparams(("arbitrary", "arbitrary"), VMEM_LIMIT),
        name="projection",
    )(x3, shift, scale, gain, wts["w_in"], wts["qa_gain"], wts["w_qb"], wts["w_uk"],
      wts["kv_gain"], tabs["cq"], tabs["sq"], tabs["ck"], tabs["sk"])


def _rope_tables(pos):
    r = MLA_ROPE_DIM
    freqs = ROPE_THETA ** (-jnp.arange(0, r, 2, dtype=F32) / r)
    ang = pos.astype(F32)[:, None] * freqs[None, :]
    cos, sin = jnp.cos(ang), jnp.sin(ang)
    zeros = jnp.zeros((pos.shape[0], LANES - r), F32)
    ck = jnp.concatenate([cos, cos, zeros], axis=-1)
    sk = jnp.concatenate([-sin, sin, zeros], axis=-1)
    return dict(cq=jnp.tile(ck, (1, MLA_HEADS))[None], sq=jnp.tile(sk, (1, MLA_HEADS))[None],
                ck=ck[None], sk=sk[None])


def _prepare_weights(w_in, q_a_norm, kv_a_norm, w_qb, w_kvb, w_out):
    d = w_in.shape[0]
    q_rank, kv_rank = w_qb.shape[0], w_kvb.shape[0]
    diff_w = d // 2
    diff_kv = diff_w // DIFF_GROUP
    r = MLA_ROPE_DIM
    widths = [("q_d", diff_w), ("k_d", diff_kv), ("v_d", diff_kv), ("g_a", diff_w),
              ("q_a", q_rank), ("kv", kv_rank), ("kpe", r), ("g_b", d - diff_w)]
    src, a = {}, 0
    for name, w in widths:
        src[name] = (a, a + w)
        a += w
    assert a == w_in.shape[1]

    def cut(name):
        return w_in[:, src[name][0]:src[name][1]]

    swap = jnp.concatenate([jnp.arange(r // 2, r), jnp.arange(0, r // 2)])
    zpad = jnp.zeros((d, LANES - r), w_in.dtype)
    kpe = cut("kpe")
    parts = [("q_d", cut("q_d")), ("k_d", cut("k_d")), ("v_d", cut("v_d")), ("q_a", cut("q_a")),
             ("kv", cut("kv")), ("kpe", jnp.concatenate([kpe, zpad], 1)),
             ("kpe_sw", jnp.concatenate([kpe[:, swap], zpad], 1))]
    cols, a = {}, 0
    for name, p in parts:
        cols[name] = (a, a + p.shape[1])
        a += p.shape[1]
    w_proj = jnp.concatenate([p for _, p in parts], axis=1).astype(BF16)
    w_gate = jnp.concatenate([cut("g_a"), cut("g_b")], axis=1).astype(BF16)

    nope = w_qb[:, :, :MLA_NOPE_DIM].reshape(q_rank, MLA_HEADS * MLA_NOPE_DIM)
    pe = w_qb[:, :, MLA_NOPE_DIM:]
    zq = jnp.zeros((q_rank, MLA_HEADS, LANES - r), w_qb.dtype)
    pe_p = jnp.concatenate([pe, zq], axis=-1).reshape(q_rank, MLA_HEADS * LANES)
    pe_sw = jnp.concatenate([pe[:, :, swap], zq], axis=-1).reshape(q_rank, MLA_HEADS * LANES)
    w_qb_aug = jnp.concatenate([nope, pe_p, pe_sw], axis=1).astype(BF16)
    w_uk = jnp.transpose(w_kvb[:, :, :MLA_NOPE_DIM], (1, 2, 0)).astype(BF16)
    w_uv = jnp.transpose(w_kvb[:, :, MLA_NOPE_DIM:], (1, 0, 2)).astype(BF16)
    return dict(cols=cols, w_in=w_proj, w_gate=w_gate, w_qb=w_qb_aug, w_uk=w_uk, w_uv=w_uv,
                qa_gain=q_a_norm[None], kv_gain=kv_a_norm[None], w_out=w_out.astype(BF16))


def _softmax_step(s, m_ref, l_ref, idx):
    m_prev = m_ref[idx]
    m_new = jnp.maximum(m_prev, jnp.max(s, axis=-1, keepdims=True))
    alpha = jnp.exp(m_prev - m_new)
    p = jnp.exp(s - m_new[:, :1])
    l_ref[idx] = alpha * l_ref[idx] + jnp.sum(p, axis=-1, keepdims=True)
    m_ref[idx] = m_new
    return p, alpha


def _lam(lp_ref, lam_init):
    lp = lp_ref[...]
    a = jnp.sum(lp[0:1] * lp[1:2], axis=-1, keepdims=True)
    b = jnp.sum(lp[2:3] * lp[3:4], axis=-1, keepdims=True)
    return jnp.exp(a) - jnp.exp(b) + lam_init


def _dot_t(a, b):
    return lax.dot_general(a, b, (((1,), (1,)), ((), ())), preferred_element_type=F32)


def _pdiff_kernel(q_ref, k_ref, v_ref, bias_ref, lp_ref, sg_ref, o_ref, m_ref, l_ref, acc_ref,
                  *, lam_init):
    qi, ki = pl.program_id(2), pl.program_id(3)
    tq, tk = q_ref.shape[0], k_ref.shape[0]
    nblk = tq // LANES

    @pl.when(ki == 0)
    def _():
        m_ref[...] = jnp.full(m_ref.shape, NEG_INF, F32)
        l_ref[...] = jnp.zeros(l_ref.shape, F32)
        acc_ref[...] = jnp.zeros(acc_ref.shape, F32)

    def bias_tile(r, diag):
        z = jnp.zeros((LANES, LANES), F32)
        same, prev = bias_ref[r, 0], bias_ref[r, 1]
        rows = []
        for i in range(nblk):
            if diag:
                blocks = [same if j == i else prev if j == i - 1 else z for j in range(nblk)]
            else:
                blocks = [prev if (i == 0 and j == nblk - 1) else z for j in range(nblk)]
            rows.append(jnp.concatenate(blocks, axis=1))
        return jnp.concatenate(rows, axis=0)

    def step(mode):
        q = q_ref[...]
        k = k_ref[...]
        v = v_ref[...]
        lane = lax.broadcasted_iota(jnp.int32, k.shape, 1)
        half = k.shape[1] // 2
        kmap = (jnp.where(lane < half, k, jnp.zeros_like(k)),
                jnp.where(lane >= half, k, jnp.zeros_like(k)))
        if mode == "diag":
            row = lax.broadcasted_iota(jnp.int32, (tq, tk), 0)
            col = lax.broadcasted_iota(jnp.int32, (tq, tk), 1)
            causal = col <= row
        for r in range(DIFF_GROUP):
            qr = q[:, r * LANES:(r + 1) * LANES]
            if mode != "full":
                bias = bias_tile(r, mode == "diag")
            for mp in range(2):
                s = _dot_t(qr, kmap[mp])
                if mode != "full":
                    s = s + bias
                if mode == "diag":
                    s = jnp.where(causal, s, NEG_INF)
                idx = r * 2 + mp
                p, alpha = _softmax_step(s, m_ref, l_ref, idx)
                acc_ref[idx] = alpha * acc_ref[idx] + jnp.dot(
                    p.astype(BF16), v, preferred_element_type=F32)

    @pl.when(ki < qi - 1)
    def _():
        step("full")

    @pl.when(ki == qi - 1)
    def _():
        step("prev")

    @pl.when(ki == qi)
    def _():
        step("diag")
        lam = _lam(lp_ref, lam_init)
        for r in range(DIFF_GROUP):
            o = (acc_ref[2 * r] / l_ref[2 * r] - lam * (acc_ref[2 * r + 1] / l_ref[2 * r + 1]))
            o_ref[:, r * LANES:(r + 1) * LANES] = _rms(o, sg_ref[...]) * (1.0 - lam_init)


def _prompt_diff_attention(qd, kd, vd, bias_p, lam_params, subln_gain, batch, seq, lam_init):
    t = ATT_TILE
    assert seq % t == 0 and t % LANES == 0
    n = seq // t
    gw = DIFF_GROUP * LANES
    grid = (batch, DIFF_KV_HEADS, n, n)
    kv_spec = pl.BlockSpec((t, LANES), lambda b, g, i, j: (b * n + jnp.minimum(i, j), g))
    return pl.pallas_call(
        functools.partial(_pdiff_kernel, lam_init=lam_init),
        grid=grid,
        in_specs=[pl.BlockSpec((t, gw), lambda b, g, i, j: (b * n + i, g)), kv_spec, kv_spec,
                  pl.BlockSpec((DIFF_GROUP, 2, LANES, LANES), lambda b, g, i, j: (g, 0, 0, 0)),
                  pl.BlockSpec(lam_params.shape, lambda b, g, i, j: (0, 0)),
                  pl.BlockSpec(subln_gain.shape, lambda b, g, i, j: (0, 0))],
        out_specs=pl.BlockSpec((t, gw), lambda b, g, i, j: (b * n + i, g)),
        out_shape=jax.ShapeDtypeStruct((batch * seq, DIFF_KV_HEADS * gw), F32),
        scratch_shapes=[pltpu.VMEM((2 * DIFF_GROUP, t, LANES), F32)] * 3,
        compiler_params=_cparams(("arbitrary",) * 4, 48 * 1024 * 1024),
        name="prompt_diff_attention",
    )(qd, kd, vd, bias_p, lam_params, subln_gain)


def _pmla_kernel(q_ref, k_ref, o_ref, m_ref, l_ref, acc_ref, *, scale):
    qi, ki = pl.program_id(1), pl.program_id(2)
    tq, tk = q_ref.shape[0], k_ref.shape[0]
    lat_w = MLA_PAD - LANES

    @pl.when(ki == 0)
    def _():
        m_ref[...] = jnp.full(m_ref.shape, NEG_INF, F32)
        l_ref[...] = jnp.zeros(l_ref.shape, F32)
        acc_ref[...] = jnp.zeros(acc_ref.shape, F32)

    def step(diag):
        k = k_ref[...]
        ckv = k[:, :lat_w]
        if diag:
            row = lax.broadcasted_iota(jnp.int32, (tq, tk), 0)
            col = lax.broadcasted_iota(jnp.int32, (tq, tk), 1)
            causal = col <= row
        for h in range(MLA_HEADS):
            s = _dot_t(q_ref[:, h * MLA_PAD:(h + 1) * MLA_PAD], k) * scale
            if diag:
                s = jnp.where(causal, s, NEG_INF)
            p, alpha = _softmax_step(s, m_ref, l_ref, h)
            acc_ref[h] = alpha[:, :1] * acc_ref[h] + jnp.dot(
                p.astype(BF16), ckv, preferred_element_type=F32)

    @pl.when(ki < qi)
    def _():
        step(False)

    @pl.when(ki == qi)
    def _():
        step(True)
        for h in range(MLA_HEADS):
            o_ref[:, h * lat_w:(h + 1) * lat_w] = (acc_ref[h] / l_ref[h][:, :1]).astype(o_ref.dtype)


def _prompt_mla_attention(qcat, kcat, batch, seq):
    t = ATT_TILE
    n = seq // t
    lat_w = MLA_PAD - LANES
    scale = (MLA_NOPE_DIM + MLA_ROPE_DIM) ** -0.5
    return pl.pallas_call(
        functools.partial(_pmla_kernel, scale=scale),
        grid=(batch, n, n),
        in_specs=[pl.BlockSpec((t, MLA_HEADS * MLA_PAD), lambda b, i, j: (b * n + i, 0)),
                  pl.BlockSpec((t, MLA_PAD), lambda b, i, j: (b * n + jnp.minimum(i, j), 0))],
        out_specs=pl.BlockSpec((t, MLA_HEADS * lat_w), lambda b, i, j: (b * n + i, 0)),
        out_shape=jax.ShapeDtypeStruct((batch * seq, MLA_HEADS * lat_w), BF16),
        scratch_shapes=[pltpu.VMEM((MLA_HEADS, t, LANES), F32), pltpu.VMEM((MLA_HEADS, t, LANES), F32),
                        pltpu.VMEM((MLA_HEADS, t, lat_w), F32)],
        compiler_params=_cparams(("arbitrary",) * 3, 48 * 1024 * 1024),
        name="prompt_mla_attention",
    )(qcat, kcat)


def _sdiff_kernel(pt_ref, q_ref, kn_ref, vn_ref, bias_ref, lp_ref, sg_ref, *refs, npg, lam_init):
    k_refs, v_refs = refs[:npg], refs[npg:2 * npg]
    o_ref, m_ref, l_ref, acc_ref = refs[2 * npg:]
    c, nc = pl.program_id(1), pl.num_programs(1)
    ng = DIFF_KV_HEADS
    rows = q_ref.shape[1]
    rg = rows // ng
    page = k_refs[0].shape[0] // ng

    @pl.when(c == 0)
    def _():
        m_ref[...] = jnp.full(m_ref.shape, NEG_INF, F32)
        l_ref[...] = jnp.zeros(l_ref.shape, F32)
        acc_ref[...] = jnp.zeros(acc_ref.shape, F32)

    q = q_ref[0]

    def group_rows(ref, g):
        return ref[pl.ds(g, page, stride=ng), :].astype(BF16)

    def update(s, value_of):
        p, alpha = _softmax_step(s, m_ref, l_ref, 0)
        p = p.astype(BF16)
        acc_ref[...] = alpha * acc_ref[...]
        for g in range(ng):
            tot = None
            for off, val in value_of(g):
                part = jnp.dot(p[g * rg:(g + 1) * rg, off:off + val.shape[0]], val,
                               preferred_element_type=F32)
                tot = part if tot is None else tot + part
            acc_ref[g * rg:(g + 1) * rg, :] += tot

    is_last = c == nc - 1
    s_pages = []
    for i in range(npg):
        sp = jnp.concatenate([_dot_t(q[g * rg:(g + 1) * rg], group_rows(k_refs[i], g))
                              for g in range(ng)], axis=0)
        if i == npg - 1:
            sp = sp + bias_ref[0] * is_last.astype(F32)
        s_pages.append(sp)
    update(jnp.concatenate(s_pages, axis=1),
           lambda g: [(i * page, group_rows(v_refs[i], g)) for i in range(npg)])

    @pl.when(is_last)
    def _():
        kn = kn_ref[0]
        vn = vn_ref[0]
        tn = kn.shape[0]
        sn = jnp.concatenate([_dot_t(q[g * rg:(g + 1) * rg], kn[:, g * LANES:(g + 1) * LANES])
                              for g in range(ng)], axis=0)
        sn = sn + bias_ref[1][:, :tn]
        update(sn, lambda g: [(0, vn[:, g * LANES:(g + 1) * LANES])])
        lam = _lam(lp_ref, lam_init)
        o = acc_ref[...] / l_ref[0]
        hr = rg // 2
        for g in range(ng):
            od = o[g * rg:g * rg + hr] - lam * o[g * rg + hr:(g + 1) * rg]
            o_ref[0, g * hr:(g + 1) * hr, :] = _rms(od, sg_ref[...]) * (1.0 - lam_init)


def _sample_diff_attention(page_table, q_bd, k_new, v_new, bias_s, lam_params, subln_gain,
                           cache_k, cache_v, lam_init):
    nseq, n_pages = page_table.shape
    npg = min(SAMPLE_PAGES_PER_STEP, n_pages)
    assert n_pages % npg == 0
    rows = q_bd.shape[1]
    tn = k_new.shape[1]
    page_rows = LANES * DIFF_KV_HEADS

    def page_spec(i):
        return pl.BlockSpec((page_rows, LANES), lambda b, c, pt: (pt[b, c * npg + i], 0))

    seq3 = lambda shape: pl.BlockSpec((1,) + shape, lambda b, c, pt: (b, 0, 0))
    const = lambda shape: pl.BlockSpec(shape, lambda b, c, pt: (0,) * len(shape))
    grid_spec = pltpu.PrefetchScalarGridSpec(
        num_scalar_prefetch=1,
        grid=(nseq, n_pages // npg),
        in_specs=[seq3((rows, LANES)), seq3((tn, k_new.shape[2])), seq3((tn, v_new.shape[2])),
                  const(bias_s.shape), const(lam_params.shape), const(subln_gain.shape)]
                 + [page_spec(i) for i in range(npg)] * 2,
        out_specs=seq3((rows // 2, LANES)),
        scratch_shapes=[pltpu.VMEM((1, rows, LANES), F32), pltpu.VMEM((1, rows, LANES), F32),
                        pltpu.VMEM((rows, LANES), F32)])
    return pl.pallas_call(
        functools.partial(_sdiff_kernel, npg=npg, lam_init=lam_init),
        grid_spec=grid_spec,
        out_shape=jax.ShapeDtypeStruct((nseq, rows // 2, LANES), F32),
        compiler_params=_cparams(("arbitrary", "arbitrary"), 48 * 1024 * 1024),
        name="sample_diff_attention",
    )(page_table, q_bd, k_new, v_new, bias_s, lam_params, subln_gain,
      *([cache_k] * npg), *([cache_v] * npg))


def _smla_kernel(pt_ref, ql_ref, qp_ref, cn_ref, pn_ref, *refs, npg, scale, tn):
    c_refs, p_refs = refs[:npg], refs[npg:2 * npg]
    o_ref, m_ref, l_ref, acc_ref = refs[2 * npg:]
    c, nc = pl.program_id(1), pl.num_programs(1)
    page = c_refs[0].shape[0]

    @pl.when(c == 0)
    def _():
        m_ref[...] = jnp.full(m_ref.shape, NEG_INF, F32)
        l_ref[...] = jnp.zeros(l_ref.shape, F32)
        acc_ref[...] = jnp.zeros(acc_ref.shape, F32)

    ql = ql_ref[0]
    qp = qp_ref[0]

    def update(s, values):
        p, alpha = _softmax_step(s, m_ref, l_ref, 0)
        p = p.astype(BF16)
        tot = None
        for off, val in values:
            part = jnp.dot(p[:, off:off + val.shape[0]], val, preferred_element_type=F32)
            tot = part if tot is None else tot + part
        acc_ref[...] = alpha[:, :1] * acc_ref[...] + tot

    ckv = [c_refs[i][...].astype(BF16) for i in range(npg)]
    s = jnp.concatenate(
        [_dot_t(ql, ckv[i]) + jnp.dot(qp, p_refs[i][...].astype(BF16), preferred_element_type=F32)
         for i in range(npg)], axis=1) * scale
    update(s, [(i * page, ckv[i]) for i in range(npg)])

    @pl.when(c == nc - 1)
    def _():
        cn = cn_ref[0]
        sn = (_dot_t(ql, cn) + _dot_t(qp, pn_ref[0])) * scale
        row = lax.broadcasted_iota(jnp.int32, sn.shape, 0)
        col = lax.broadcasted_iota(jnp.int32, sn.shape, 1)
        sn = jnp.where(col <= jnp.bitwise_and(row, tn - 1), sn, NEG_INF)
        update(sn, [(0, cn)])
        o_ref[0] = (acc_ref[...] / l_ref[0][:, :1]).astype(o_ref.dtype)


def _sample_mla_attention(page_table, q_lat, q_pe, ckv_new, kpe_new, cache_ckv, cache_kpe_t):
    nseq, n_pages = page_table.shape
    npg = min(SAMPLE_PAGES_PER_STEP, n_pages)
    assert n_pages % npg == 0
    rows, lat_w = q_lat.shape[1:]
    rope = q_pe.shape[2]
    tn = ckv_new.shape[1]
    dec_seq = rows // MLA_HEADS
    assert dec_seq & (dec_seq - 1) == 0
    page = cache_ckv.shape[0] // (cache_kpe_t.shape[0] // rope)
    scale = (MLA_NOPE_DIM + MLA_ROPE_DIM) ** -0.5

    def ckv_spec(i):
        return pl.BlockSpec((page, lat_w), lambda b, c, pt: (pt[b, c * npg + i], 0))

    def kpe_spec(i):
        return pl.BlockSpec((rope, page), lambda b, c, pt: (pt[b, c * npg + i], 0))

    seq3 = lambda shape: pl.BlockSpec((1,) + shape, lambda b, c, pt: (b, 0, 0))
    grid_spec = pltpu.PrefetchScalarGridSpec(
        num_scalar_prefetch=1,
        grid=(nseq, n_pages // npg),
        in_specs=[seq3((rows, lat_w)), seq3((rows, rope)), seq3((tn, lat_w)), seq3((tn, rope))]
                 + [ckv_spec(i) for i in range(npg)] + [kpe_spec(i) for i in range(npg)],
        out_specs=seq3((rows, lat_w)),
        scratch_shapes=[pltpu.VMEM((1, rows, LANES), F32), pltpu.VMEM((1, rows, LANES), F32),
                        pltpu.VMEM((rows, lat_w), F32)])
    return pl.pallas_call(
        functools.partial(_smla_kernel, npg=npg, scale=scale, tn=dec_seq),
        grid_spec=grid_spec,
        out_shape=jax.ShapeDtypeStruct((nseq, rows, lat_w), BF16),
        compiler_params=_cparams(("arbitrary", "arbitrary"), 48 * 1024 * 1024),
        name="sample_mla_attention",
    )(page_table, q_lat, q_pe, ckv_new, kpe_new, *([cache_ckv] * npg), *([cache_kpe_t] * npg))


def _merge_kernel(x_ref, sh_ref, sc_ref, gt_ref, gain_ref, wg_ref, od_ref, ol_ref, wuv_ref,
                  wo_ref, fn_ref, y_ref):
    nb, tt, d = x_ref.shape
    tm = nb * tt
    h = _modulated(x_ref, gain_ref, sc_ref, sh_ref)
    g = jnp.dot(h, wg_ref[...], preferred_element_type=F32)
    dw = od_ref.shape[1]
    ga, gb = g[:, :dw], g[:, dw:]
    ua = (od_ref[...] * (ga * jax.nn.sigmoid(ga))).astype(BF16)
    lat_w = wuv_ref.shape[1]
    vd = wuv_ref.shape[2]
    om = jnp.concatenate(
        [jnp.dot(ol_ref[:, hh * lat_w:(hh + 1) * lat_w], wuv_ref[hh], preferred_element_type=F32)
         for hh in range(MLA_HEADS)], axis=1)
    ub = (om * (gb * jax.nn.sigmoid(gb))).astype(BF16)
    y = (jnp.dot(ua, wo_ref[:dw, :], preferred_element_type=F32)
         + jnp.dot(ub, wo_ref[dw:, :], preferred_element_type=F32))
    xn = x_ref[...] + gt_ref[...] * y.reshape(nb, tt, d)
    y_ref[...] = _rms(xn, fn_ref[...])


def _merge(x3, shift, scale, gate, gain, o_d, o_lat, wts, final_norm):
    nseq, tt, d = x3.shape
    nb, tmt = _row_tiling(nseq, tt)
    tm = nb * tmt
    nt = tt // tmt

    def rowblk(width):
        return pl.BlockSpec((tm, width), lambda i, j: (i * nt + j, 0))

    def seqblk():
        return pl.BlockSpec((nb, 1, d), lambda i, j: (i, 0, 0))

    xblk = pl.BlockSpec((nb, tmt, d), lambda i, j: (i, j, 0))
    return pl.pallas_call(
        _merge_kernel,
        grid=(nseq // nb, nt),
        in_specs=[xblk, seqblk(), seqblk(), seqblk(), _resident((1, d)),
                  _resident(wts["w_gate"].shape), rowblk(o_d.shape[1]), rowblk(o_lat.shape[1]),
                  _resident(wts["w_uv"].shape), _resident(wts["w_out"].shape), _resident((1, d))],
        out_specs=xblk,
        out_shape=jax.ShapeDtypeStruct(x3.shape, F32),
        compiler_params=_cparams(("arbitrary", "arbitrary"), VMEM_LIMIT),
        name="merge",
    )(x3, shift, scale, gate, gain, wts["w_gate"], o_d, o_lat, wts["w_uv"], wts["w_out"],
      final_norm)


def kernel(x_prompt, x_sample, c_prompt, c_sample, cache_diff_k, cache_diff_v, cache_mla_ckv,
           cache_mla_kpe, page_table, w_mod, b_mod, norm_gain, w_in, lambda_params, subln_gain,
           q_a_norm, kv_a_norm, w_qb, w_kvb, w_out, rel_bias, final_norm):
    depth = w_mod.shape[0]
    assert depth == 1, "single-layer trunk"
    batch, seq, d = x_prompt.shape
    nseq, dec_seq, _ = x_sample.shape
    n_pool, page = cache_mla_ckv.shape[1:3]
    past_len = page_table.shape[1] * page
    lam_init = 0.8 - 0.6 * math.exp(-0.3 * 0)
    ng, dv = cache_diff_k.shape[3:]
    lat_w = cache_mla_ckv.shape[3]
    rope = cache_mla_kpe.shape[3]

    wts = _prepare_weights(w_in[0], q_a_norm[0], kv_a_norm[0], w_qb[0], w_kvb[0], w_out[0])
    gain = norm_gain[0][None]
    fnorm = final_norm[None]
    lam_params = lambda_params[0]
    sgain = subln_gain[0][None]

    mod = _modulation(jnp.concatenate([c_prompt, c_sample], axis=0), w_mod[0], b_mod)
    shift, scale, gate = (mod[:, i * d:(i + 1) * d][:, None, :] for i in range(3))
    bias_p, bias_s = _bias_tiles(rel_bias, dec_seq)

    tabs_p = _rope_tables(jnp.arange(seq, dtype=jnp.int32))
    (qd, kd32, vd32, kd16, vd16, qcat, ckv32, kpe32, kcat) = _projection(
        x_prompt, shift[:batch], scale[:batch], gain, wts, tabs_p)
    o_d = _prompt_diff_attention(qd, kd16, vd16, bias_p, lam_params, sgain, batch, seq, lam_init)
    o_lat = _prompt_mla_attention(qcat, kcat, batch, seq)
    y_prompt = _merge(x_prompt, shift[:batch], scale[:batch], gate[:batch], gain, o_d, o_lat,
                      wts, fnorm)
    new_p = (kd32.reshape(1, batch, seq, ng, dv), vd32.reshape(1, batch, seq, ng, dv),
             ckv32.reshape(1, batch, seq, lat_w), kpe32.reshape(1, batch, seq, rope))

    tabs_s = _rope_tables(past_len + jnp.arange(dec_seq, dtype=jnp.int32))
    (qd, kd32, vd32, kd16, vd16, qcat, ckv32, kpe32, kcat) = _projection(
        x_sample, shift[batch:], scale[batch:], gain, wts, tabs_s)
    hd = dv // 2
    q6 = qd.reshape(nseq, dec_seq, ng, DIFF_GROUP, 2, hd)
    q6 = jnp.transpose(q6, (0, 2, 4, 3, 1, 5))
    zq = jnp.zeros_like(q6[:, :, 0])
    q_bd = jnp.stack([jnp.concatenate([q6[:, :, 0], zq], -1),
                      jnp.concatenate([zq, q6[:, :, 1]], -1)], axis=2)
    q_bd = q_bd.reshape(nseq, ng * 2 * DIFF_GROUP * dec_seq, dv)

    def as_page(a):
        a = a.reshape(nseq, dec_seq, -1)
        return jnp.pad(a, ((0, 0), (0, page - dec_seq), (0, 0)))

    o_ds = _sample_diff_attention(
        page_table, q_bd, as_page(kd16), as_page(vd16),
        bias_s, lam_params, sgain, cache_diff_k.reshape(n_pool * page * ng, dv),
        cache_diff_v.reshape(n_pool * page * ng, dv), lam_init)
    o_ds = jnp.transpose(o_ds.reshape(nseq, ng, DIFF_GROUP, dec_seq, dv), (0, 3, 1, 2, 4))
    o_ds = o_ds.reshape(nseq * dec_seq, ng * DIFF_GROUP * dv)

    q4 = jnp.transpose(qcat.reshape(nseq, dec_seq, MLA_HEADS, MLA_PAD), (0, 2, 1, 3))
    q4 = q4.reshape(nseq, MLA_HEADS * dec_seq, MLA_PAD)
    kc = as_page(kcat)
    o_ls = _sample_mla_attention(
        page_table, q4[:, :, :lat_w], q4[:, :, lat_w:lat_w + rope], kc[:, :, :lat_w],
        kc[:, :, lat_w:lat_w + rope], cache_mla_ckv.reshape(n_pool * page, lat_w),
        jnp.swapaxes(cache_mla_kpe[0], 1, 2).reshape(n_pool * rope, page))
    o_ls = jnp.transpose(o_ls.reshape(nseq, MLA_HEADS, dec_seq, lat_w), (0, 2, 1, 3))
    o_ls = o_ls.reshape(nseq * dec_seq, MLA_HEADS * lat_w)
    y_sample = _merge(x_sample, shift[batch:], scale[batch:], gate[batch:], gain, o_ds, o_ls,
                      wts, fnorm)
    new_s = (kd32.reshape(1, nseq, dec_seq, ng, dv), vd32.reshape(1, nseq, dec_seq, ng, dv),
             ckv32.reshape(1, nseq, dec_seq, lat_w), kpe32.reshape(1, nseq, dec_seq, rope))
    return (y_prompt, y_sample) + new_p + new_s
```

```python
import functools
import math

import numpy as np
import jax
import jax.numpy as jnp
from jax import lax
from jax.experimental import pallas as pl
from jax.experimental.pallas import tpu as pltpu

F32 = jnp.float32
BF16 = jnp.bfloat16

DIFF_HEADS = 8
DIFF_KV_HEADS = 4
DIFF_GROUP = DIFF_HEADS // DIFF_KV_HEADS
MLA_HEADS = 8
MLA_NOPE_DIM = 128
MLA_ROPE_DIM = 64
ROPE_THETA = 10000.0
REL_BUCKETS = 32
REL_MAX_EXACT = REL_BUCKETS // 2
REL_MAX_DIST = 128
EPS = 1e-6
NEG_INF = -1e30

LANES = 128
V7X_VMEM_BYTES = 64 * 1024 * 1024
VMEM_LIMIT = 56 * 1024 * 1024
ROW_TILE = 256
ATT_TILE = 512
MLA_PAD = 3 * LANES
SAMPLE_PAGES_PER_STEP = 16


def _cparams(sem, vmem=None):
    return pltpu.CompilerParams(dimension_semantics=sem, vmem_limit_bytes=vmem)


def _resident(shape):
    nd = len(shape)
    return pl.BlockSpec(shape, lambda *_: (0,) * nd, pipeline_mode=pl.Buffered(1))


def _rel_bucket(n):
    n = np.asarray(n, np.int64)
    nf = np.maximum(n, 1).astype(np.float32)
    large = REL_MAX_EXACT + (
        np.log(nf / np.float32(REL_MAX_EXACT)) / np.float32(math.log(REL_MAX_DIST / REL_MAX_EXACT))
        * np.float32(REL_BUCKETS - REL_MAX_EXACT)).astype(np.int32)
    large = np.minimum(large, REL_BUCKETS - 1)
    return np.where(n < REL_MAX_EXACT, n, large).astype(np.int32)


def _mod_kernel(c_ref, w_ref, b_ref, o_ref):
    c = c_ref[...]
    a = (c * jax.nn.sigmoid(c)).astype(BF16)
    o_ref[...] = jnp.dot(a, w_ref[...].astype(BF16), preferred_element_type=F32) + b_ref[...]


def _modulation(c, w_mod, b_mod):
    rows, d = c.shape
    n = w_mod.shape[1]
    tn = 1536
    return pl.pallas_call(
        _mod_kernel,
        grid=(n // tn,),
        in_specs=[pl.BlockSpec((rows, d), lambda j: (0, 0)),
                  pl.BlockSpec((d, tn), lambda j: (0, j)),
                  pl.BlockSpec((1, tn), lambda j: (0, j))],
        out_specs=pl.BlockSpec((rows, tn), lambda j: (0, j)),
        out_shape=jax.ShapeDtypeStruct((rows, n), F32),
        compiler_params=_cparams(("arbitrary",), 48 * 1024 * 1024),
        name="modulation",
    )(c, w_mod, b_mod)


def _bias_kernel(tab_ref, bp_ref, bs_ref, hs_ref, op_ref, os_ref):
    far = REL_BUCKETS - 1
    for h in range(DIFF_HEADS):
        for v in range(2):
            bkt = bp_ref[v]
            acc = jnp.zeros(bkt.shape, F32)
            for b in range(far):
                acc = jnp.where(bkt == b, tab_ref[b, h] - tab_ref[far, h], acc)
            op_ref[h, v] = acc
    head = hs_ref[...]
    for v in range(2):
        bkt = bs_ref[v]
        acc = jnp.zeros(bkt.shape, F32)
        for b in range(far):
            val = jnp.zeros(bkt.shape, F32)
            for h in range(DIFF_HEADS):
                val = jnp.where(head == h, tab_ref[b, h] - tab_ref[far, h], val)
            acc = jnp.where(bkt == b, val, acc)
        os_ref[v] = jnp.where(bkt == REL_BUCKETS, NEG_INF, acc)


def _bias_tiles(rel_bias, dec_seq):
    i = np.arange(LANES)[:, None]
    j = np.arange(LANES)[None, :]
    bp = np.stack([_rel_bucket(np.maximum(i - j, 0)), _rel_bucket(LANES + i - j)])
    t = i % dec_seq
    head = ((i // (4 * dec_seq)) * DIFF_GROUP + (i // dec_seq) % DIFF_GROUP) + 0 * j
    last = _rel_bucket(LANES + t - j)
    new = np.where(j <= t, _rel_bucket(np.maximum(t - j, 0)), REL_BUCKETS)
    bs = np.stack([last, new])
    vm = pl.BlockSpec(memory_space=pltpu.VMEM)
    return pl.pallas_call(
        _bias_kernel,
        in_specs=[pl.BlockSpec(memory_space=pltpu.SMEM), vm, vm, vm],
        out_specs=[vm, vm],
        out_shape=[jax.ShapeDtypeStruct((DIFF_HEADS, 2, LANES, LANES), F32),
                   jax.ShapeDtypeStruct((2, LANES, LANES), F32)],
        name="rel_bias_tiles",
    )(rel_bias, jnp.asarray(bp, jnp.int32), jnp.asarray(bs, jnp.int32),
      jnp.asarray(head, jnp.int32))


def _rms(x, gain):
    return x * lax.rsqrt(jnp.mean(x * x, axis=-1, keepdims=True) + EPS) * gain


def _modulated(x_ref, gain_ref, sc_ref, sh_ref):
    nb, tt, d = x_ref.shape
    h = _rms(x_ref[...], gain_ref[...]) * (1.0 + sc_ref[...]) + sh_ref[...]
    return h.reshape(nb * tt, d).astype(BF16)


def _proj_kernel(x_ref, sh_ref, sc_ref, gain_ref, w_ref, qag_ref, wqb_ref, wuk_ref, kvg_ref,
                 cq_ref, sq_ref, ck_ref, sk_ref,
                 qd_ref, kd32_ref, vd32_ref, kd16_ref, vd16_ref, qcat_ref, ckv32_ref, kpe32_ref,
                 kcat_ref, *, cols):
    nb, tt, _ = x_ref.shape
    tm = nb * tt
    h = _modulated(x_ref, gain_ref, sc_ref, sh_ref)

    def seg(name):
        a, b = cols[name]
        return jnp.dot(h, w_ref[:, a:b], preferred_element_type=F32)

    qd_ref[...] = (seg("q_d") * 0.125).astype(BF16)
    kd = seg("k_d")
    kd32_ref[...] = kd
    kd16_ref[...] = kd.astype(BF16)
    vd = seg("v_d")
    vd32_ref[...] = vd
    vd16_ref[...] = vd.astype(BF16)

    qn = _rms(seg("q_a"), qag_ref[...]).astype(BF16)
    nope_w = MLA_HEADS * MLA_NOPE_DIM
    pe_w = MLA_HEADS * LANES
    qall = jnp.dot(qn, wqb_ref[...], preferred_element_type=F32)
    pe = qall[:, nope_w:nope_w + pe_w].reshape(nb, tt, pe_w)
    pe_sw = qall[:, nope_w + pe_w:].reshape(nb, tt, pe_w)
    qpe = (pe * cq_ref[...] + pe_sw * sq_ref[...]).reshape(tm, pe_w).astype(BF16)
    lat_w = MLA_PAD - LANES
    for hh in range(MLA_HEADS):
        nope = qall[:, hh * MLA_NOPE_DIM:(hh + 1) * MLA_NOPE_DIM].astype(BF16)
        ql = jnp.dot(nope, wuk_ref[hh], preferred_element_type=F32)
        qcat_ref[:, hh * MLA_PAD:hh * MLA_PAD + lat_w] = ql.astype(BF16)
        qcat_ref[:, hh * MLA_PAD + lat_w:(hh + 1) * MLA_PAD] = qpe[:, hh * LANES:(hh + 1) * LANES]

    ckv = _rms(seg("kv"), kvg_ref[...])
    ckv32_ref[...] = ckv
    kcat_ref[:, :lat_w] = ckv.astype(BF16)
    kp = (seg("kpe").reshape(nb, tt, LANES) * ck_ref[...]
          + seg("kpe_sw").reshape(nb, tt, LANES) * sk_ref[...]).reshape(tm, LANES)
    kpe32_ref[...] = kp[:, :MLA_ROPE_DIM]
    kcat_ref[:, lat_w:] = kp.astype(BF16)


def _row_tiling(nseq, tt):
    if tt >= ROW_TILE:
        assert tt % ROW_TILE == 0
        return 1, ROW_TILE
    nb = min(ROW_TILE // tt, nseq)
    assert nseq % nb == 0
    return nb, tt


def _projection(x3, shift, scale, gain, wts, tabs):
    nseq, tt, d = x3.shape
    nb, tmt = _row_tiling(nseq, tt)
    tm = nb * tmt
    rows = nseq * tt
    nt = tt // tmt
    grid = (nseq // nb, nt)

    def rowblk(width):
        return pl.BlockSpec((tm, width), lambda i, j: (i * nt + j, 0))

    def seqblk():
        return pl.BlockSpec((nb, 1, d), lambda i, j: (i, 0, 0))

    def tabblk(width):
        return pl.BlockSpec((1, tmt, width), lambda i, j: (0, j, 0))

    diff_q = wts["cols"]["q_d"][1] - wts["cols"]["q_d"][0]
    diff_kv = wts["cols"]["k_d"][1] - wts["cols"]["k_d"][0]
    lat_w = MLA_PAD - LANES
    pe_w = MLA_HEADS * LANES
    out_shapes = [
        ((rows, diff_q), BF16), ((rows, diff_kv), F32), ((rows, diff_kv), F32),
        ((rows, diff_kv), BF16), ((rows, diff_kv), BF16), ((rows, MLA_HEADS * MLA_PAD), BF16),
        ((rows, lat_w), F32), ((rows, MLA_ROPE_DIM), F32), ((rows, MLA_PAD), BF16)]
    return pl.pallas_call(
        functools.partial(_proj_kernel, cols=wts["cols"]),
        grid=grid,
        in_specs=[pl.BlockSpec((nb, tmt, d), lambda i, j: (i, j, 0)), seqblk(), seqblk(),
                  _resident((1, d)), _resident(wts["w_in"].shape), _resident((1, wts["qa_gain"].shape[1])),
                  _resident(wts["w_qb"].shape), _resident(wts["w_uk"].shape),
                  _resident((1, lat_w)),
                  tabblk(pe_w), tabblk(pe_w), tabblk(LANES), tabblk(LANES)],
        out_specs=[rowblk(s[1]) for s, _ in out_shapes],
        out_shape=[jax.ShapeDtypeStruct(s, dt) for s, dt in out_shapes],
        compiler_params=_cparams(("arbitrary", "arbitrary"), VMEM_LIMIT),
        name="projection",
    )(x3, shift, scale, gain, wts["w_in"], wts["qa_gain"], wts["w_qb"], wts["w_uk"],
      wts["kv_gain"], tabs["cq"], tabs["sq"], tabs["ck"], tabs["sk"])


def _rope_tables(pos):
    r = MLA_ROPE_DIM
    freqs = ROPE_THETA ** (-jnp.arange(0, r, 2, dtype=F32) / r)
    ang = pos.astype(F32)[:, None] * freqs[None, :]
    cos, sin = jnp.cos(ang), jnp.sin(ang)
    zeros = jnp.zeros((pos.shape[0], LANES - r), F32)
    ck = jnp.concatenate([cos, cos, zeros], axis=-1)
    sk = jnp.concatenate([-sin, sin, zeros], axis=-1)
    return dict(cq=jnp.tile(ck, (1, MLA_HEADS))[None], sq=jnp.tile(sk, (1, MLA_HEADS))[None],
                ck=ck[None], sk=sk[None])


def _prepare_weights(w_in, q_a_norm, kv_a_norm, w_qb, w_kvb, w_out):
    d = w_in.shape[0]
    q_rank, kv_rank = w_qb.shape[0], w_kvb.shape[0]
    diff_w = d // 2
    diff_kv = diff_w // DIFF_GROUP
    r = MLA_ROPE_DIM
    widths = [("q_d", diff_w), ("k_d", diff_kv), ("v_d", diff_kv), ("g_a", diff_w),
              ("q_a", q_rank), ("kv", kv_rank), ("kpe", r), ("g_b", d - diff_w)]
    src, a = {}, 0
    for name, w in widths:
        src[name] = (a, a + w)
        a += w
    assert a == w_in.shape[1]

    def cut(name):
        return w_in[:, src[name][0]:src[name][1]]

    swap = jnp.concatenate([jnp.arange(r // 2, r), jnp.arange(0, r // 2)])
    zpad = jnp.zeros((d, LANES - r), w_in.dtype)
    kpe = cut("kpe")
    parts = [("q_d", cut("q_d")), ("k_d", cut("k_d")), ("v_d", cut("v_d")), ("q_a", cut("q_a")),
             ("kv", cut("kv")), ("kpe", jnp.concatenate([kpe, zpad], 1)),
             ("kpe_sw", jnp.concatenate([kpe[:, swap], zpad], 1))]
    cols, a = {}, 0
    for name, p in parts:
        cols[name] = (a, a + p.shape[1])
        a += p.shape[1]
    w_proj = jnp.concatenate([p for _, p in parts], axis=1).astype(BF16)
    w_gate = jnp.concatenate([cut("g_a"), cut("g_b")], axis=1).astype(BF16)

    nope = w_qb[:, :, :MLA_NOPE_DIM].reshape(q_rank, MLA_HEADS * MLA_NOPE_DIM)
    pe = w_qb[:, :, MLA_NOPE_DIM:]
    zq = jnp.zeros((q_rank, MLA_HEADS, LANES - r), w_qb.dtype)
    pe_p = jnp.concatenate([pe, zq], axis=-1).reshape(q_rank, MLA_HEADS * LANES)
    pe_sw = jnp.concatenate([pe[:, :, swap], zq], axis=-1).reshape(q_rank, MLA_HEADS * LANES)
    w_qb_aug = jnp.concatenate([nope, pe_p, pe_sw], axis=1).astype(BF16)
    w_uk = jnp.transpose(w_kvb[:, :, :MLA_NOPE_DIM], (1, 2, 0)).astype(BF16)
    w_uv = jnp.transpose(w_kvb[:, :, MLA_NOPE_DIM:], (1, 0, 2)).astype(BF16)
    return dict(cols=cols, w_in=w_proj, w_gate=w_gate, w_qb=w_qb_aug, w_uk=w_uk, w_uv=w_uv,
                qa_gain=q_a_norm[None], kv_gain=kv_a_norm[None], w_out=w_out.astype(BF16))


def _softmax_step(s, m_ref, l_ref, idx):
    m_prev = m_ref[idx]
    m_new = jnp.maximum(m_prev, jnp.max(s, axis=-1, keepdims=True))
    alpha = jnp.exp(m_prev - m_new)
    p = jnp.exp(s - m_new[:, :1])
    l_ref[idx] = alpha * l_ref[idx] + jnp.sum(p, axis=-1, keepdims=True)
    m_ref[idx] = m_new
    return p, alpha


def _lam(lp_ref, lam_init):
    lp = lp_ref[...]
    a = jnp.sum(lp[0:1] * lp[1:2], axis=-1, keepdims=True)
    b = jnp.sum(lp[2:3] * lp[3:4], axis=-1, keepdims=True)
    return jnp.exp(a) - jnp.exp(b) + lam_init


def _dot_t(a, b):
    return lax.dot_general(a, b, (((1,), (1,)), ((), ())), preferred_element_type=F32)


def _pdiff_kernel(q_ref, k_ref, v_ref, bias_ref, lp_ref, sg_ref, o_ref, m_ref, l_ref, acc_ref,
                  *, lam_init):
    qi, ki = pl.program_id(2), pl.program_id(3)
    tq, tk = q_ref.shape[0], k_ref.shape[0]
    nblk = tq // LANES

    @pl.when(ki == 0)
    def _():
        m_ref[...] = jnp.full(m_ref.shape, NEG_INF, F32)
        l_ref[...] = jnp.zeros(l_ref.shape, F32)
        acc_ref[...] = jnp.zeros(acc_ref.shape, F32)

    def bias_tile(r, diag):
        z = jnp.zeros((LANES, LANES), F32)
        same, prev = bias_ref[r, 0], bias_ref[r, 1]
        rows = []
        for i in range(nblk):
            if diag:
                blocks = [same if j == i else prev if j == i - 1 else z for j in range(nblk)]
            else:
                blocks = [prev if (i == 0 and j == nblk - 1) else z for j in range(nblk)]
            rows.append(jnp.concatenate(blocks, axis=1))
        return jnp.concatenate(rows, axis=0)

    def step(mode):
        q = q_ref[...]
        k = k_ref[...]
        v = v_ref[...]
        lane = lax.broadcasted_iota(jnp.int32, k.shape, 1)
        half = k.shape[1] // 2
        kmap = (jnp.where(lane < half, k, jnp.zeros_like(k)),
                jnp.where(lane >= half, k, jnp.zeros_like(k)))
        if mode == "diag":
            row = lax.broadcasted_iota(jnp.int32, (tq, tk), 0)
            col = lax.broadcasted_iota(jnp.int32, (tq, tk), 1)
            causal = col <= row
        for r in range(DIFF_GROUP):
            qr = q[:, r * LANES:(r + 1) * LANES]
            if mode != "full":
                bias = bias_tile(r, mode == "diag")
            for mp in range(2):
                s = _dot_t(qr, kmap[mp])
                if mode != "full":
                    s = s + bias
                if mode == "diag":
                    s = jnp.where(causal, s, NEG_INF)
                idx = r * 2 + mp
                p, alpha = _softmax_step(s, m_ref, l_ref, idx)
                acc_ref[idx] = alpha * acc_ref[idx] + jnp.dot(
                    p.astype(BF16), v, preferred_element_type=F32)

    @pl.when(ki < qi - 1)
    def _():
        step("full")

    @pl.when(ki == qi - 1)
    def _():
        step("prev")

    @pl.when(ki == qi)
    def _():
        step("diag")
        lam = _lam(lp_ref, lam_init)
        for r in range(DIFF_GROUP):
            o = (acc_ref[2 * r] / l_ref[2 * r] - lam * (acc_ref[2 * r + 1] / l_ref[2 * r + 1]))
            o_ref[:, r * LANES:(r + 1) * LANES] = _rms(o, sg_ref[...]) * (1.0 - lam_init)


def _prompt_diff_attention(qd, kd, vd, bias_p, lam_params, subln_gain, batch, seq, lam_init):
    t = ATT_TILE
    assert seq % t == 0 and t % LANES == 0
    n = seq // t
    gw = DIFF_GROUP * LANES
    grid = (batch, DIFF_KV_HEADS, n, n)
    kv_spec = pl.BlockSpec((t, LANES), lambda b, g, i, j: (b * n + jnp.minimum(i, j), g))
    return pl.pallas_call(
        functools.partial(_pdiff_kernel, lam_init=lam_init),
        grid=grid,
        in_specs=[pl.BlockSpec((t, gw), lambda b, g, i, j: (b * n + i, g)), kv_spec, kv_spec,
                  pl.BlockSpec((DIFF_GROUP, 2, LANES, LANES), lambda b, g, i, j: (g, 0, 0, 0)),
                  pl.BlockSpec(lam_params.shape, lambda b, g, i, j: (0, 0)),
                  pl.BlockSpec(subln_gain.shape, lambda b, g, i, j: (0, 0))],
        out_specs=pl.BlockSpec((t, gw), lambda b, g, i, j: (b * n + i, g)),
        out_shape=jax.ShapeDtypeStruct((batch * seq, DIFF_KV_HEADS * gw), F32),
        scratch_shapes=[pltpu.VMEM((2 * DIFF_GROUP, t, LANES), F32)] * 3,
        compiler_params=_cparams(("arbitrary",) * 4, 48 * 1024 * 1024),
        name="prompt_diff_attention",
    )(qd, kd, vd, bias_p, lam_params, subln_gain)


def _pmla_kernel(q_ref, k_ref, o_ref, m_ref, l_ref, acc_ref, *, scale):
    qi, ki = pl.program_id(1), pl.program_id(2)
    tq, tk = q_ref.shape[0], k_ref.shape[0]
    lat_w = MLA_PAD - LANES

    @pl.when(ki == 0)
    def _():
        m_ref[...] = jnp.full(m_ref.shape, NEG_INF, F32)
        l_ref[...] = jnp.zeros(l_ref.shape, F32)
        acc_ref[...] = jnp.zeros(acc_ref.shape, F32)

    def step(diag):
        k = k_ref[...]
        ckv = k[:, :lat_w]
        if diag:
            row = lax.broadcasted_iota(jnp.int32, (tq, tk), 0)
            col = lax.broadcasted_iota(jnp.int32, (tq, tk), 1)
            causal = col <= row
        for h in range(MLA_HEADS):
            s = _dot_t(q_ref[:, h * MLA_PAD:(h + 1) * MLA_PAD], k) * scale
            if diag:
                s = jnp.where(causal, s, NEG_INF)
            p, alpha = _softmax_step(s, m_ref, l_ref, h)
            acc_ref[h] = alpha[:, :1] * acc_ref[h] + jnp.dot(
                p.astype(BF16), ckv, preferred_element_type=F32)

    @pl.when(ki < qi)
    def _():
        step(False)

    @pl.when(ki == qi)
    def _():
        step(True)
        for h in range(MLA_HEADS):
            o_ref[:, h * lat_w:(h + 1) * lat_w] = (acc_ref[h] / l_ref[h][:, :1]).astype(o_ref.dtype)


def _prompt_mla_attention(qcat, kcat, batch, seq):
    t = ATT_TILE
    n = seq // t
    lat_w = MLA_PAD - LANES
    scale = (MLA_NOPE_DIM + MLA_ROPE_DIM) ** -0.5
    return pl.pallas_call(
        functools.partial(_pmla_kernel, scale=scale),
        grid=(batch, n, n),
        in_specs=[pl.BlockSpec((t, MLA_HEADS * MLA_PAD), lambda b, i, j: (b * n + i, 0)),
                  pl.BlockSpec((t, MLA_PAD), lambda b, i, j: (b * n + jnp.minimum(i, j), 0))],
        out_specs=pl.BlockSpec((t, MLA_HEADS * lat_w), lambda b, i, j: (b * n + i, 0)),
        out_shape=jax.ShapeDtypeStruct((batch * seq, MLA_HEADS * lat_w), BF16),
        scratch_shapes=[pltpu.VMEM((MLA_HEADS, t, LANES), F32), pltpu.VMEM((MLA_HEADS, t, LANES), F32),
                        pltpu.VMEM((MLA_HEADS, t, lat_w), F32)],
        compiler_params=_cparams(("arbitrary",) * 3, 48 * 1024 * 1024),
        name="prompt_mla_attention",
    )(qcat, kcat)


def _sample_kernel(pt_ref, q_ref, kn_ref, vn_ref, bias_ref, lp_ref, sg_ref, qm_ref, cn_ref, *refs,
                   npg, lam_init, scale, dec_seq):
    k_refs, v_refs, c_refs, p_refs = (refs[i * npg:(i + 1) * npg] for i in range(4))
    od_ref, ol_ref, md_ref, ld_ref, accd_ref, ml_ref, ll_ref, accl_ref = refs[4 * npg:]
    c, nc = pl.program_id(1), pl.num_programs(1)
    is_last = c == nc - 1
    ng = DIFF_KV_HEADS
    rows = q_ref.shape[1]
    rg = rows // ng
    page, lat_w = c_refs[0].shape
    rope = p_refs[0].shape[0]

    @pl.when(c == 0)
    def _():
        for m_ref, l_ref, acc_ref in ((md_ref, ld_ref, accd_ref), (ml_ref, ll_ref, accl_ref)):
            m_ref[...] = jnp.full(m_ref.shape, NEG_INF, F32)
            l_ref[...] = jnp.zeros(l_ref.shape, F32)
            acc_ref[...] = jnp.zeros(acc_ref.shape, F32)

    q = q_ref[0]
    ql = qm_ref[0, :, :lat_w]
    qp = qm_ref[0, :, lat_w:]

    def group_rows(ref, g):
        return ref[pl.ds(g, page, stride=ng), :].astype(BF16)

    def diff_update(s, value_of):
        p, alpha = _softmax_step(s, md_ref, ld_ref, 0)
        p = p.astype(BF16)
        accd_ref[...] = alpha * accd_ref[...]
        for g in range(ng):
            tot = None
            for off, val in value_of(g):
                part = jnp.dot(p[g * rg:(g + 1) * rg, off:off + val.shape[0]], val,
                               preferred_element_type=F32)
                tot = part if tot is None else tot + part
            accd_ref[g * rg:(g + 1) * rg, :] += tot

    def mla_update(s, values):
        p, alpha = _softmax_step(s, ml_ref, ll_ref, 0)
        p = p.astype(BF16)
        tot = None
        for off, val in values:
            part = jnp.dot(p[:, off:off + val.shape[0]], val, preferred_element_type=F32)
            tot = part if tot is None else tot + part
        accl_ref[...] = alpha[:, :1] * accl_ref[...] + tot

    s_pages = []
    for i in range(npg):
        sp = jnp.concatenate([_dot_t(q[g * rg:(g + 1) * rg], group_rows(k_refs[i], g))
                              for g in range(ng)], axis=0)
        if i == npg - 1:
            sp = sp + bias_ref[0] * is_last.astype(F32)
        s_pages.append(sp)
    ckv = [c_refs[i][...].astype(BF16) for i in range(npg)]
    s_lat = jnp.concatenate(
        [_dot_t(ql, ckv[i]) + jnp.dot(qp[:, :rope], p_refs[i][...].astype(BF16),
                                      preferred_element_type=F32)
         for i in range(npg)], axis=1) * scale
    diff_update(jnp.concatenate(s_pages, axis=1),
                lambda g: [(i * page, group_rows(v_refs[i], g)) for i in range(npg)])
    mla_update(s_lat, [(i * page, ckv[i]) for i in range(npg)])

    @pl.when(is_last)
    def _():
        kn = kn_ref[0]
        vn = vn_ref[0]
        sn = jnp.concatenate([_dot_t(q[g * rg:(g + 1) * rg], kn[:, g * LANES:(g + 1) * LANES])
                              for g in range(ng)], axis=0)
        diff_update(sn + bias_ref[1], lambda g: [(0, vn[:, g * LANES:(g + 1) * LANES])])
        cn = cn_ref[0]
        sm = (_dot_t(ql, cn[:, :lat_w]) + _dot_t(qp, cn[:, lat_w:])) * scale
        row = lax.broadcasted_iota(jnp.int32, sm.shape, 0)
        col = lax.broadcasted_iota(jnp.int32, sm.shape, 1)
        sm = jnp.where(col <= jnp.bitwise_and(row, dec_seq - 1), sm, NEG_INF)
        mla_update(sm, [(0, cn[:, :lat_w])])

        lam = _lam(lp_ref, lam_init)
        o = accd_ref[...] / ld_ref[0]
        hr = rg // 2
        for g in range(ng):
            od = o[g * rg:g * rg + hr] - lam * o[g * rg + hr:(g + 1) * rg]
            od_ref[0, g * hr:(g + 1) * hr, :] = _rms(od, sg_ref[...]) * (1.0 - lam_init)
        ol_ref[0] = (accl_ref[...] / ll_ref[0][:, :1]).astype(ol_ref.dtype)


def _sample_attention(page_table, q_bd, k_new, v_new, bias_s, lam_params, subln_gain, q_mla,
                      kcat_new, cache_k, cache_v, cache_ckv, cache_kpe_t, lam_init):
    nseq, n_pages = page_table.shape
    npg = min(SAMPLE_PAGES_PER_STEP, n_pages)
    assert n_pages % npg == 0
    rows = q_bd.shape[1]
    mrows = q_mla.shape[1]
    dec_seq = mrows // MLA_HEADS
    assert dec_seq & (dec_seq - 1) == 0
    lat_w = cache_ckv.shape[1]
    page = k_new.shape[1]
    rope = cache_kpe_t.shape[0] // (cache_ckv.shape[0] // page)
    scale = (MLA_NOPE_DIM + MLA_ROPE_DIM) ** -0.5

    def page_spec(rows_, width, i):
        return pl.BlockSpec((rows_, width), lambda b, c, pt: (pt[b, c * npg + i], 0))

    def pages(rows_, width):
        return [page_spec(rows_, width, i) for i in range(npg)]

    seq3 = lambda a: pl.BlockSpec((1,) + a.shape[1:], lambda b, c, pt: (b, 0, 0))
    const = lambda a: pl.BlockSpec(a.shape, lambda b, c, pt: (0,) * a.ndim)
    stat = lambda r: pltpu.VMEM((1, r, LANES), F32)
    grid_spec = pltpu.PrefetchScalarGridSpec(
        num_scalar_prefetch=1,
        grid=(nseq, n_pages // npg),
        in_specs=[seq3(q_bd), seq3(k_new), seq3(v_new), const(bias_s), const(lam_params),
                  const(subln_gain), seq3(q_mla), seq3(kcat_new)]
                 + pages(page * DIFF_KV_HEADS, LANES) * 2 + pages(page, lat_w) + pages(rope, page),
        out_specs=[pl.BlockSpec((1, rows // 2, LANES), lambda b, c, pt: (b, 0, 0)),
                   pl.BlockSpec((1, mrows, lat_w), lambda b, c, pt: (b, 0, 0))],
        scratch_shapes=[stat(rows), stat(rows), pltpu.VMEM((rows, LANES), F32),
                        stat(mrows), stat(mrows), pltpu.VMEM((mrows, lat_w), F32)])
    return pl.pallas_call(
        functools.partial(_sample_kernel, npg=npg, lam_init=lam_init, scale=scale,
                          dec_seq=dec_seq),
        grid_spec=grid_spec,
        out_shape=[jax.ShapeDtypeStruct((nseq, rows // 2, LANES), F32),
                   jax.ShapeDtypeStruct((nseq, mrows, lat_w), BF16)],
        compiler_params=_cparams(("arbitrary", "arbitrary"), 48 * 1024 * 1024),
        name="sample_attention",
    )(page_table, q_bd, k_new, v_new, bias_s, lam_params, subln_gain, q_mla, kcat_new,
      *([cache_k] * npg), *([cache_v] * npg), *([cache_ckv] * npg), *([cache_kpe_t] * npg))


def _merge_kernel(x_ref, sh_ref, sc_ref, gt_ref, gain_ref, wg_ref, od_ref, ol_ref, wuv_ref,
                  wo_ref, fn_ref, y_ref):
    nb, tt, d = x_ref.shape
    tm = nb * tt
    h = _modulated(x_ref, gain_ref, sc_ref, sh_ref)
    g = jnp.dot(h, wg_ref[...], preferred_element_type=F32)
    dw = od_ref.shape[1]
    ga, gb = g[:, :dw], g[:, dw:]
    ua = (od_ref[...] * (ga * jax.nn.sigmoid(ga))).astype(BF16)
    lat_w = wuv_ref.shape[1]
    vd = wuv_ref.shape[2]
    om = jnp.concatenate(
        [jnp.dot(ol_ref[:, hh * lat_w:(hh + 1) * lat_w], wuv_ref[hh], preferred_element_type=F32)
         for hh in range(MLA_HEADS)], axis=1)
    ub = (om * (gb * jax.nn.sigmoid(gb))).astype(BF16)
    y = (jnp.dot(ua, wo_ref[:dw, :], preferred_element_type=F32)
         + jnp.dot(ub, wo_ref[dw:, :], preferred_element_type=F32))
    xn = x_ref[...] + gt_ref[...] * y.reshape(nb, tt, d)
    y_ref[...] = _rms(xn, fn_ref[...])


def _merge(x3, shift, scale, gate, gain, o_d, o_lat, wts, final_norm):
    nseq, tt, d = x3.shape
    nb, tmt = _row_tiling(nseq, tt)
    tm = nb * tmt
    nt = tt // tmt

    def rowblk(width):
        return pl.BlockSpec((tm, width), lambda i, j: (i * nt + j, 0))

    def seqblk():
        return pl.BlockSpec((nb, 1, d), lambda i, j: (i, 0, 0))

    xblk = pl.BlockSpec((nb, tmt, d), lambda i, j: (i, j, 0))
    return pl.pallas_call(
        _merge_kernel,
        grid=(nseq // nb, nt),
        in_specs=[xblk, seqblk(), seqblk(), seqblk(), _resident((1, d)),
                  _resident(wts["w_gate"].shape), rowblk(o_d.shape[1]), rowblk(o_lat.shape[1]),
                  _resident(wts["w_uv"].shape), _resident(wts["w_out"].shape), _resident((1, d))],
        out_specs=xblk,
        out_shape=jax.ShapeDtypeStruct(x3.shape, F32),
        compiler_params=_cparams(("arbitrary", "arbitrary"), VMEM_LIMIT),
        name="merge",
    )(x3, shift, scale, gate, gain, wts["w_gate"], o_d, o_lat, wts["w_uv"], wts["w_out"],
      final_norm)


def kernel(x_prompt, x_sample, c_prompt, c_sample, cache_diff_k, cache_diff_v, cache_mla_ckv,
           cache_mla_kpe, page_table, w_mod, b_mod, norm_gain, w_in, lambda_params, subln_gain,
           q_a_norm, kv_a_norm, w_qb, w_kvb, w_out, rel_bias, final_norm):
    depth = w_mod.shape[0]
    assert depth == 1, "single-layer trunk"
    batch, seq, d = x_prompt.shape
    nseq, dec_seq, _ = x_sample.shape
    n_pool, page = cache_mla_ckv.shape[1:3]
    past_len = page_table.shape[1] * page
    lam_init = 0.8 - 0.6 * math.exp(-0.3 * 0)
    ng, dv = cache_diff_k.shape[3:]
    lat_w = cache_mla_ckv.shape[3]
    rope = cache_mla_kpe.shape[3]

    wts = _prepare_weights(w_in[0], q_a_norm[0], kv_a_norm[0], w_qb[0], w_kvb[0], w_out[0])
    gain = norm_gain[0][None]
    fnorm = final_norm[None]
    lam_params = lambda_params[0]
    sgain = subln_gain[0][None]

    mod = _modulation(jnp.concatenate([c_prompt, c_sample], axis=0), w_mod[0], b_mod)
    shift, scale, gate = (mod[:, i * d:(i + 1) * d][:, None, :] for i in range(3))
    bias_p, bias_s = _bias_tiles(rel_bias, dec_seq)

    tabs_p = _rope_tables(jnp.arange(seq, dtype=jnp.int32))
    (qd, kd32, vd32, kd16, vd16, qcat, ckv32, kpe32, kcat) = _projection(
        x_prompt, shift[:batch], scale[:batch], gain, wts, tabs_p)
    o_d = _prompt_diff_attention(qd, kd16, vd16, bias_p, lam_params, sgain, batch, seq, lam_init)
    o_lat = _prompt_mla_attention(qcat, kcat, batch, seq)
    y_prompt = _merge(x_prompt, shift[:batch], scale[:batch], gate[:batch], gain, o_d, o_lat,
                      wts, fnorm)
    new_p = (kd32.reshape(1, batch, seq, ng, dv), vd32.reshape(1, batch, seq, ng, dv),
             ckv32.reshape(1, batch, seq, lat_w), kpe32.reshape(1, batch, seq, rope))

    tabs_s = _rope_tables(past_len + jnp.arange(dec_seq, dtype=jnp.int32))
    (qd, kd32, vd32, kd16, vd16, qcat, ckv32, kpe32, kcat) = _projection(
        x_sample, shift[batch:], scale[batch:], gain, wts, tabs_s)
    hd = dv // 2
    q6 = qd.reshape(nseq, dec_seq, ng, DIFF_GROUP, 2, hd)
    q6 = jnp.transpose(q6, (0, 2, 4, 3, 1, 5))
    zq = jnp.zeros_like(q6[:, :, 0])
    q_bd = jnp.stack([jnp.concatenate([q6[:, :, 0], zq], -1),
                      jnp.concatenate([zq, q6[:, :, 1]], -1)], axis=2)
    q_bd = q_bd.reshape(nseq, ng * 2 * DIFF_GROUP * dec_seq, dv)

    def as_page(a):
        a = a.reshape(nseq, dec_seq, -1)
        return jnp.pad(a, ((0, 0), (0, page - dec_seq), (0, 0)))

    q4 = jnp.transpose(qcat.reshape(nseq, dec_seq, MLA_HEADS, MLA_PAD), (0, 2, 1, 3))
    q4 = q4.reshape(nseq, MLA_HEADS * dec_seq, MLA_PAD)
    o_ds, o_ls = _sample_attention(
        page_table, q_bd, as_page(kd16), as_page(vd16), bias_s, lam_params, sgain, q4,
        as_page(kcat), cache_diff_k.reshape(n_pool * page * ng, dv),
        cache_diff_v.reshape(n_pool * page * ng, dv), cache_mla_ckv.reshape(n_pool * page, lat_w),
        jnp.swapaxes(cache_mla_kpe[0], 1, 2).reshape(n_pool * rope, page), lam_init)
    o_ds = jnp.transpose(o_ds.reshape(nseq, ng, DIFF_GROUP, dec_seq, dv), (0, 3, 1, 2, 4))
    o_ds = o_ds.reshape(nseq * dec_seq, ng * DIFF_GROUP * dv)
    o_ls = jnp.transpose(o_ls.reshape(nseq, MLA_HEADS, dec_seq, lat_w), (0, 2, 1, 3))
    o_ls = o_ls.reshape(nseq * dec_seq, MLA_HEADS * lat_w)
    y_sample = _merge(x_sample, shift[batch:], scale[batch:], gate[batch:], gain, o_ds, o_ls,
                      wts, fnorm)
    new_s = (kd32.reshape(1, nseq, dec_seq, ng, dv), vd32.reshape(1, nseq, dec_seq, ng, dv),
             ckv32.reshape(1, nseq, dec_seq, lat_w), kpe32.reshape(1, nseq, dec_seq, rope))
    return (y_prompt, y_sample) + new_p + new_s
```

```python
import functools
import math

import numpy as np
import jax
import jax.numpy as jnp
from jax import lax
from jax.experimental import pallas as pl
from jax.experimental.pallas import tpu as pltpu

F32 = jnp.float32
BF16 = jnp.bfloat16

DIFF_HEADS = 8
DIFF_KV_HEADS = 4
DIFF_GROUP = DIFF_HEADS // DIFF_KV_HEADS
MLA_HEADS = 8
MLA_NOPE_DIM = 128
MLA_ROPE_DIM = 64
ROPE_THETA = 10000.0
REL_BUCKETS = 32
REL_MAX_EXACT = REL_BUCKETS // 2
REL_MAX_DIST = 128
EPS = 1e-6
NEG_INF = -1e30

LANES = 128
V7X_VMEM_BYTES = 64 * 1024 * 1024
VMEM_LIMIT = 56 * 1024 * 1024
ROW_TILE = 256
ATT_TILE = 512
MLA_PAD = 3 * LANES
SAMPLE_PAGES_PER_STEP = 16


def _cparams(sem, vmem=None):
    return pltpu.CompilerParams(dimension_semantics=sem, vmem_limit_bytes=vmem)


def _resident(shape):
    nd = len(shape)
    return pl.BlockSpec(shape, lambda *_: (0,) * nd, pipeline_mode=pl.Buffered(1))


def _rel_bucket(n):
    n = np.asarray(n, np.int64)
    nf = np.maximum(n, 1).astype(np.float32)
    large = REL_MAX_EXACT + (
        np.log(nf / np.float32(REL_MAX_EXACT)) / np.float32(math.log(REL_MAX_DIST / REL_MAX_EXACT))
        * np.float32(REL_BUCKETS - REL_MAX_EXACT)).astype(np.int32)
    large = np.minimum(large, REL_BUCKETS - 1)
    return np.where(n < REL_MAX_EXACT, n, large).astype(np.int32)


def _mod_kernel(c_ref, w_ref, b_ref, o_ref):
    c = c_ref[...]
    a = (c * jax.nn.sigmoid(c)).astype(BF16)
    o_ref[...] = jnp.dot(a, w_ref[...].astype(BF16), preferred_element_type=F32) + b_ref[...]


def _modulation(c, w_mod, b_mod):
    rows, d = c.shape
    n = w_mod.shape[1]
    tn = 1536
    return pl.pallas_call(
        _mod_kernel,
        grid=(n // tn,),
        in_specs=[pl.BlockSpec((rows, d), lambda j: (0, 0)),
                  pl.BlockSpec((d, tn), lambda j: (0, j)),
                  pl.BlockSpec((1, tn), lambda j: (0, j))],
        out_specs=pl.BlockSpec((rows, tn), lambda j: (0, j)),
        out_shape=jax.ShapeDtypeStruct((rows, n), F32),
        compiler_params=_cparams(("arbitrary",), 48 * 1024 * 1024),
        name="modulation",
    )(c, w_mod, b_mod)


def _bias_kernel(tab_ref, bp_ref, bs_ref, hs_ref, op_ref, os_ref):
    far = REL_BUCKETS - 1
    for h in range(DIFF_HEADS):
        for v in range(2):
            bkt = bp_ref[v]
            acc = jnp.zeros(bkt.shape, F32)
            for b in range(far):
                acc = jnp.where(bkt == b, tab_ref[b, h] - tab_ref[far, h], acc)
            op_ref[h, v] = acc
    head = hs_ref[...]
    for v in range(2):
        bkt = bs_ref[v]
        acc = jnp.zeros(bkt.shape, F32)
        for b in range(far):
            val = jnp.zeros(bkt.shape, F32)
            for h in range(DIFF_HEADS):
                val = jnp.where(head == h, tab_ref[b, h] - tab_ref[far, h], val)
            acc = jnp.where(bkt == b, val, acc)
        os_ref[v] = jnp.where(bkt == REL_BUCKETS, NEG_INF, acc)


def _bias_tiles(rel_bias, dec_seq):
    i = np.arange(LANES)[:, None]
    j = np.arange(LANES)[None, :]
    bp = np.stack([_rel_bucket(np.maximum(i - j, 0)), _rel_bucket(LANES + i - j)])
    t = i % dec_seq
    head = ((i // (4 * dec_seq)) * DIFF_GROUP + (i // dec_seq) % DIFF_GROUP) + 0 * j
    last = _rel_bucket(LANES + t - j)
    new = np.where(j <= t, _rel_bucket(np.maximum(t - j, 0)), REL_BUCKETS)
    bs = np.stack([last, new])
    vm = pl.BlockSpec(memory_space=pltpu.VMEM)
    return pl.pallas_call(
        _bias_kernel,
        in_specs=[pl.BlockSpec(memory_space=pltpu.SMEM), vm, vm, vm],
        out_specs=[vm, vm],
        out_shape=[jax.ShapeDtypeStruct((DIFF_HEADS, 2, LANES, LANES), F32),
                   jax.ShapeDtypeStruct((2, LANES, LANES), F32)],
        name="rel_bias_tiles",
    )(rel_bias, jnp.asarray(bp, jnp.int32), jnp.asarray(bs, jnp.int32),
      jnp.asarray(head, jnp.int32))


def _rms(x, gain):
    return x * lax.rsqrt(jnp.mean(x * x, axis=-1, keepdims=True) + EPS) * gain


def _modulated(x_ref, gain_ref, sc_ref, sh_ref):
    nb, tt, d = x_ref.shape
    h = _rms(x_ref[...], gain_ref[...]) * (1.0 + sc_ref[...]) + sh_ref[...]
    return h.reshape(nb * tt, d).astype(BF16)


def _proj_kernel(x_ref, sh_ref, sc_ref, gain_ref, w_ref, qag_ref, wqb_ref, wuk_ref, kvg_ref,
                 cq_ref, sq_ref, ck_ref, sk_ref,
                 qd_ref, kd32_ref, vd32_ref, kd16_ref, vd16_ref, qcat_ref, ckv32_ref, kpe32_ref,
                 kcat_ref, *, cols):
    nb, tt, _ = x_ref.shape
    tm = nb * tt
    h = _modulated(x_ref, gain_ref, sc_ref, sh_ref)

    def seg(name):
        a, b = cols[name]
        return jnp.dot(h, w_ref[:, a:b], preferred_element_type=F32)

    qd_ref[...] = (seg("q_d") * 0.125).astype(BF16)
    kd = seg("k_d")
    kd32_ref[...] = kd
    kd16_ref[...] = kd.astype(BF16)
    vd = seg("v_d")
    vd32_ref[...] = vd
    vd16_ref[...] = vd.astype(BF16)

    qn = _rms(seg("q_a"), qag_ref[...]).astype(BF16)
    nope_w = MLA_HEADS * MLA_NOPE_DIM
    pe_w = MLA_HEADS * LANES
    qall = jnp.dot(qn, wqb_ref[...], preferred_element_type=F32)
    pe = qall[:, nope_w:nope_w + pe_w].reshape(nb, tt, pe_w)
    pe_sw = qall[:, nope_w + pe_w:].reshape(nb, tt, pe_w)
    qpe = (pe * cq_ref[...] + pe_sw * sq_ref[...]).reshape(tm, pe_w).astype(BF16)
    lat_w = MLA_PAD - LANES
    for hh in range(MLA_HEADS):
        nope = qall[:, hh * MLA_NOPE_DIM:(hh + 1) * MLA_NOPE_DIM].astype(BF16)
        ql = jnp.dot(nope, wuk_ref[hh], preferred_element_type=F32)
        qcat_ref[:, hh * MLA_PAD:hh * MLA_PAD + lat_w] = ql.astype(BF16)
        qcat_ref[:, hh * MLA_PAD + lat_w:(hh + 1) * MLA_PAD] = qpe[:, hh * LANES:(hh + 1) * LANES]

    ckv = _rms(seg("kv"), kvg_ref[...])
    ckv32_ref[...] = ckv
    kcat_ref[:, :lat_w] = ckv.astype(BF16)
    kp = (seg("kpe").reshape(nb, tt, LANES) * ck_ref[...]
          + seg("kpe_sw").reshape(nb, tt, LANES) * sk_ref[...]).reshape(tm, LANES)
    kpe32_ref[...] = kp[:, :MLA_ROPE_DIM]
    kcat_ref[:, lat_w:] = kp.astype(BF16)


def _row_tiling(nseq, tt):
    if tt >= ROW_TILE:
        assert tt % ROW_TILE == 0
        return 1, ROW_TILE
    nb = min(ROW_TILE // tt, nseq)
    assert nseq % nb == 0
    return nb, tt


def _projection(x3, shift, scale, gain, wts, tabs):
    nseq, tt, d = x3.shape
    nb, tmt = _row_tiling(nseq, tt)
    tm = nb * tmt
    rows = nseq * tt
    nt = tt // tmt
    grid = (nseq // nb, nt)

    def rowblk(width):
        return pl.BlockSpec((tm, width), lambda i, j: (i * nt + j, 0))

    def seqblk():
        return pl.BlockSpec((nb, 1, d), lambda i, j: (i, 0, 0))

    def tabblk(width):
        return pl.BlockSpec((1, tmt, width), lambda i, j: (0, j, 0))

    diff_q = wts["cols"]["q_d"][1] - wts["cols"]["q_d"][0]
    diff_kv = wts["cols"]["k_d"][1] - wts["cols"]["k_d"][0]
    lat_w = MLA_PAD - LANES
    pe_w = MLA_HEADS * LANES
    out_shapes = [
        ((rows, diff_q), BF16), ((rows, diff_kv), F32), ((rows, diff_kv), F32),
        ((rows, diff_kv), BF16), ((rows, diff_kv), BF16), ((rows, MLA_HEADS * MLA_PAD), BF16),
        ((rows, lat_w), F32), ((rows, MLA_ROPE_DIM), F32), ((rows, MLA_PAD), BF16)]
    return pl.pallas_call(
        functools.partial(_proj_kernel, cols=wts["cols"]),
        grid=grid,
        in_specs=[pl.BlockSpec((nb, tmt, d), lambda i, j: (i, j, 0)), seqblk(), seqblk(),
                  _resident((1, d)), _resident(wts["w_in"].shape), _resident((1, wts["qa_gain"].shape[1])),
                  _resident(wts["w_qb"].shape), _resident(wts["w_uk"].shape),
                  _resident((1, lat_w)),
                  tabblk(pe_w), tabblk(pe_w), tabblk(LANES), tabblk(LANES)],
        out_specs=[rowblk(s[1]) for s, _ in out_shapes],
        out_shape=[jax.ShapeDtypeStruct(s, dt) for s, dt in out_shapes],
        compiler_params=_cparams(("arbitrary", "arbitrary"), VMEM_LIMIT),
        name="projection",
    )(x3, shift, scale, gain, wts["w_in"], wts["qa_gain"], wts["w_qb"], wts["w_uk"],
      wts["kv_gain"], tabs["cq"], tabs["sq"], tabs["ck"], tabs["sk"])


def _rope_tables(pos):
    r = MLA_ROPE_DIM
    freqs = ROPE_THETA ** (-jnp.arange(0, r, 2, dtype=F32) / r)
    ang = pos.astype(F32)[:, None] * freqs[None, :]
    cos, sin = jnp.cos(ang), jnp.sin(ang)
    zeros = jnp.zeros((pos.shape[0], LANES - r), F32)
    ck = jnp.concatenate([cos, cos, zeros], axis=-1)
    sk = jnp.concatenate([-sin, sin, zeros], axis=-1)
    return dict(cq=jnp.tile(ck, (1, MLA_HEADS))[None], sq=jnp.tile(sk, (1, MLA_HEADS))[None],
                ck=ck[None], sk=sk[None])


def _prepare_weights(w_in, q_a_norm, kv_a_norm, w_qb, w_kvb, w_out):
    d = w_in.shape[0]
    q_rank, kv_rank = w_qb.shape[0], w_kvb.shape[0]
    diff_w = d // 2
    diff_kv = diff_w // DIFF_GROUP
    r = MLA_ROPE_DIM
    widths = [("q_d", diff_w), ("k_d", diff_kv), ("v_d", diff_kv), ("g_a", diff_w),
              ("q_a", q_rank), ("kv", kv_rank), ("kpe", r), ("g_b", d - diff_w)]
    src, a = {}, 0
    for name, w in widths:
        src[name] = (a, a + w)
        a += w
    assert a == w_in.shape[1]

    def cut(name):
        return w_in[:, src[name][0]:src[name][1]]

    swap = jnp.concatenate([jnp.arange(r // 2, r), jnp.arange(0, r // 2)])
    zpad = jnp.zeros((d, LANES - r), w_in.dtype)
    kpe = cut("kpe")
    parts = [("q_d", cut("q_d")), ("k_d", cut("k_d")), ("v_d", cut("v_d")), ("q_a", cut("q_a")),
             ("kv", cut("kv")), ("kpe", jnp.concatenate([kpe, zpad], 1)),
             ("kpe_sw", jnp.concatenate([kpe[:, swap], zpad], 1))]
    cols, a = {}, 0
    for name, p in parts:
        cols[name] = (a, a + p.shape[1])
        a += p.shape[1]
    w_proj = jnp.concatenate([p for _, p in parts], axis=1).astype(BF16)
    w_gate = jnp.concatenate([cut("g_a"), cut("g_b")], axis=1).astype(BF16)

    nope = w_qb[:, :, :MLA_NOPE_DIM].reshape(q_rank, MLA_HEADS * MLA_NOPE_DIM)
    pe = w_qb[:, :, MLA_NOPE_DIM:]
    zq = jnp.zeros((q_rank, MLA_HEADS, LANES - r), w_qb.dtype)
    pe_p = jnp.concatenate([pe, zq], axis=-1).reshape(q_rank, MLA_HEADS * LANES)
    pe_sw = jnp.concatenate([pe[:, :, swap], zq], axis=-1).reshape(q_rank, MLA_HEADS * LANES)
    w_qb_aug = jnp.concatenate([nope, pe_p, pe_sw], axis=1).astype(BF16)
    w_uk = jnp.transpose(w_kvb[:, :, :MLA_NOPE_DIM], (1, 2, 0)).astype(BF16)
    w_uv = jnp.transpose(w_kvb[:, :, MLA_NOPE_DIM:], (1, 0, 2)).astype(BF16)
    return dict(cols=cols, w_in=w_proj, w_gate=w_gate, w_qb=w_qb_aug, w_uk=w_uk, w_uv=w_uv,
                qa_gain=q_a_norm[None], kv_gain=kv_a_norm[None], w_out=w_out.astype(BF16))


def _softmax_step(s, m_ref, l_ref, idx):
    m_prev = m_ref[idx]
    m_new = jnp.maximum(m_prev, jnp.max(s, axis=-1, keepdims=True))
    alpha = jnp.exp(m_prev - m_new)
    p = jnp.exp(s - m_new[:, :1])
    l_ref[idx] = alpha * l_ref[idx] + jnp.sum(p, axis=-1, keepdims=True)
    m_ref[idx] = m_new
    return p, alpha


def _lam(lp_ref, lam_init):
    lp = lp_ref[...]
    a = jnp.sum(lp[0:1] * lp[1:2], axis=-1, keepdims=True)
    b = jnp.sum(lp[2:3] * lp[3:4], axis=-1, keepdims=True)
    return jnp.exp(a) - jnp.exp(b) + lam_init


def _dot_t(a, b):
    return lax.dot_general(a, b, (((1,), (1,)), ((), ())), preferred_element_type=F32)


def _pdiff_kernel(q_ref, k_ref, v_ref, bias_ref, lp_ref, sg_ref, o_ref, m_ref, l_ref, acc_ref,
                  *, lam_init):
    qi, ki = pl.program_id(2), pl.program_id(3)
    tq, tk = q_ref.shape[0], k_ref.shape[0]
    nblk = tq // LANES

    @pl.when(ki == 0)
    def _():
        m_ref[...] = jnp.full(m_ref.shape, NEG_INF, F32)
        l_ref[...] = jnp.zeros(l_ref.shape, F32)
        acc_ref[...] = jnp.zeros(acc_ref.shape, F32)

    def bias_tile(r, diag):
        z = jnp.zeros((LANES, LANES), F32)
        same, prev = bias_ref[r, 0], bias_ref[r, 1]
        rows = []
        for i in range(nblk):
            if diag:
                blocks = [same if j == i else prev if j == i - 1 else z for j in range(nblk)]
            else:
                blocks = [prev if (i == 0 and j == nblk - 1) else z for j in range(nblk)]
            rows.append(jnp.concatenate(blocks, axis=1))
        return jnp.concatenate(rows, axis=0)

    def step(mode):
        q = q_ref[...]
        k = k_ref[...]
        v = v_ref[...]
        lane = lax.broadcasted_iota(jnp.int32, k.shape, 1)
        half = k.shape[1] // 2
        kmap = (jnp.where(lane < half, k, jnp.zeros_like(k)),
                jnp.where(lane >= half, k, jnp.zeros_like(k)))
        if mode == "diag":
            row = lax.broadcasted_iota(jnp.int32, (tq, tk), 0)
            col = lax.broadcasted_iota(jnp.int32, (tq, tk), 1)
            causal = col <= row
        for r in range(DIFF_GROUP):
            qr = q[:, r * LANES:(r + 1) * LANES]
            if mode != "full":
                bias = bias_tile(r, mode == "diag")
            for mp in range(2):
                s = _dot_t(qr, kmap[mp])
                if mode != "full":
                    s = s + bias
                if mode == "diag":
                    s = jnp.where(causal, s, NEG_INF)
                idx = r * 2 + mp
                p, alpha = _softmax_step(s, m_ref, l_ref, idx)
                acc_ref[idx] = alpha * acc_ref[idx] + jnp.dot(
                    p.astype(BF16), v, preferred_element_type=F32)

    @pl.when(ki < qi - 1)
    def _():
        step("full")

    @pl.when(ki == qi - 1)
    def _():
        step("prev")

    @pl.when(ki == qi)
    def _():
        step("diag")
        lam = _lam(lp_ref, lam_init)
        for r in range(DIFF_GROUP):
            o = (acc_ref[2 * r] / l_ref[2 * r] - lam * (acc_ref[2 * r + 1] / l_ref[2 * r + 1]))
            o_ref[:, r * LANES:(r + 1) * LANES] = _rms(o, sg_ref[...]) * (1.0 - lam_init)


def _prompt_diff_attention(qd, kd, vd, bias_p, lam_params, subln_gain, batch, seq, lam_init):
    t = ATT_TILE
    assert seq % t == 0 and t % LANES == 0
    n = seq // t
    gw = DIFF_GROUP * LANES
    grid = (batch, DIFF_KV_HEADS, n, n)
    kv_spec = pl.BlockSpec((t, LANES), lambda b, g, i, j: (b * n + jnp.minimum(i, j), g))
    return pl.pallas_call(
        functools.partial(_pdiff_kernel, lam_init=lam_init),
        grid=grid,
        in_specs=[pl.BlockSpec((t, gw), lambda b, g, i, j: (b * n + i, g)), kv_spec, kv_spec,
                  pl.BlockSpec((DIFF_GROUP, 2, LANES, LANES), lambda b, g, i, j: (g, 0, 0, 0)),
                  pl.BlockSpec(lam_params.shape, lambda b, g, i, j: (0, 0)),
                  pl.BlockSpec(subln_gain.shape, lambda b, g, i, j: (0, 0))],
        out_specs=pl.BlockSpec((t, gw), lambda b, g, i, j: (b * n + i, g)),
        out_shape=jax.ShapeDtypeStruct((batch * seq, DIFF_KV_HEADS * gw), F32),
        scratch_shapes=[pltpu.VMEM((2 * DIFF_GROUP, t, LANES), F32)] * 3,
        compiler_params=_cparams(("arbitrary",) * 4, 48 * 1024 * 1024),
        name="prompt_diff_attention",
    )(qd, kd, vd, bias_p, lam_params, subln_gain)


def _pmla_kernel(q_ref, k_ref, o_ref, m_ref, l_ref, acc_ref, *, scale):
    qi, ki = pl.program_id(1), pl.program_id(2)
    tq, tk = q_ref.shape[0], k_ref.shape[0]
    lat_w = MLA_PAD - LANES

    @pl.when(ki == 0)
    def _():
        m_ref[...] = jnp.full(m_ref.shape, NEG_INF, F32)
        l_ref[...] = jnp.zeros(l_ref.shape, F32)
        acc_ref[...] = jnp.zeros(acc_ref.shape, F32)

    def step(diag):
        k = k_ref[...]
        ckv = k[:, :lat_w]
        if diag:
            row = lax.broadcasted_iota(jnp.int32, (tq, tk), 0)
            col = lax.broadcasted_iota(jnp.int32, (tq, tk), 1)
            causal = col <= row
        for h in range(MLA_HEADS):
            s = _dot_t(q_ref[:, h * MLA_PAD:(h + 1) * MLA_PAD], k) * scale
            if diag:
                s = jnp.where(causal, s, NEG_INF)
            p, alpha = _softmax_step(s, m_ref, l_ref, h)
            acc_ref[h] = alpha[:, :1] * acc_ref[h] + jnp.dot(
                p.astype(BF16), ckv, preferred_element_type=F32)

    @pl.when(ki < qi)
    def _():
        step(False)

    @pl.when(ki == qi)
    def _():
        step(True)
        for h in range(MLA_HEADS):
            o_ref[:, h * lat_w:(h + 1) * lat_w] = (acc_ref[h] / l_ref[h][:, :1]).astype(o_ref.dtype)


def _prompt_mla_attention(qcat, kcat, batch, seq):
    t = ATT_TILE
    n = seq // t
    lat_w = MLA_PAD - LANES
    scale = (MLA_NOPE_DIM + MLA_ROPE_DIM) ** -0.5
    return pl.pallas_call(
        functools.partial(_pmla_kernel, scale=scale),
        grid=(batch, n, n),
        in_specs=[pl.BlockSpec((t, MLA_HEADS * MLA_PAD), lambda b, i, j: (b * n + i, 0)),
                  pl.BlockSpec((t, MLA_PAD), lambda b, i, j: (b * n + jnp.minimum(i, j), 0))],
        out_specs=pl.BlockSpec((t, MLA_HEADS * lat_w), lambda b, i, j: (b * n + i, 0)),
        out_shape=jax.ShapeDtypeStruct((batch * seq, MLA_HEADS * lat_w), BF16),
        scratch_shapes=[pltpu.VMEM((MLA_HEADS, t, LANES), F32), pltpu.VMEM((MLA_HEADS, t, LANES), F32),
                        pltpu.VMEM((MLA_HEADS, t, lat_w), F32)],
        compiler_params=_cparams(("arbitrary",) * 3, 48 * 1024 * 1024),
        name="prompt_mla_attention",
    )(qcat, kcat)


def _sample_kernel(pt_ref, q_ref, kn_ref, vn_ref, bias_ref, lp_ref, sg_ref, qm_ref, cn_ref,
                   k_hbm, v_hbm, c_hbm, p_hbm, od_ref, ol_ref, kbuf, vbuf, cbuf, pbuf, sem,
                   md_ref, ld_ref, accd_ref, ml_ref, ll_ref, accl_ref,
                   *, npg, lam_init, scale, dec_seq):
    b, c = pl.program_id(0), pl.program_id(1)
    nc = pl.num_programs(1)
    total = pl.num_programs(0) * nc
    step = b * nc + c
    slot = lax.rem(step, 2)
    is_last = c == nc - 1
    ng = DIFF_KV_HEADS
    rows = q_ref.shape[1]
    rg = rows // ng
    page, lat_w = cbuf.shape[2:]
    rope = pbuf.shape[2]
    streams = ((k_hbm, kbuf), (v_hbm, vbuf), (c_hbm, cbuf), (p_hbm, pbuf))

    def page_copies(step_, slot_):
        bb = lax.div(step_, nc)
        cc = lax.rem(step_, nc)
        out = []
        for i in range(npg):
            pg = pt_ref[bb, cc * npg + i]
            for j, (hbm, buf) in enumerate(streams):
                n = buf.shape[2]
                src = hbm.at[pl.ds(pl.multiple_of(pg * n, n), n), :]
                out.append(pltpu.make_async_copy(src, buf.at[slot_, i], sem.at[slot_, j]))
        return out

    @pl.when(step == 0)
    def _():
        for cp in page_copies(step, slot):
            cp.start()

    @pl.when(c == 0)
    def _():
        for m_ref, l_ref, acc_ref in ((md_ref, ld_ref, accd_ref), (ml_ref, ll_ref, accl_ref)):
            m_ref[...] = jnp.full(m_ref.shape, NEG_INF, F32)
            l_ref[...] = jnp.zeros(l_ref.shape, F32)
            acc_ref[...] = jnp.zeros(acc_ref.shape, F32)

    for cp in page_copies(step, slot):
        cp.wait()
    nxt = jnp.minimum(step + 1, total - 1)
    for cp in page_copies(nxt, 1 - slot):
        cp.start()
    k_refs, v_refs, c_refs, p_refs = ([buf.at[slot, i] for i in range(npg)] for _, buf in streams)

    q = q_ref[0]
    ql = qm_ref[0, :, :lat_w]
    qp = qm_ref[0, :, lat_w:]

    def group_rows(ref, g):
        return ref[pl.ds(g, page, stride=ng), :].astype(BF16)

    def diff_update(s, value_of):
        p, alpha = _softmax_step(s, md_ref, ld_ref, 0)
        p = p.astype(BF16)
        accd_ref[...] = alpha * accd_ref[...]
        for g in range(ng):
            tot = None
            for off, val in value_of(g):
                part = jnp.dot(p[g * rg:(g + 1) * rg, off:off + val.shape[0]], val,
                               preferred_element_type=F32)
                tot = part if tot is None else tot + part
            accd_ref[g * rg:(g + 1) * rg, :] += tot

    def mla_update(s, values):
        p, alpha = _softmax_step(s, ml_ref, ll_ref, 0)
        p = p.astype(BF16)
        tot = None
        for off, val in values:
            part = jnp.dot(p[:, off:off + val.shape[0]], val, preferred_element_type=F32)
            tot = part if tot is None else tot + part
        accl_ref[...] = alpha[:, :1] * accl_ref[...] + tot

    s_pages = []
    for i in range(npg):
        sp = jnp.concatenate([_dot_t(q[g * rg:(g + 1) * rg], group_rows(k_refs[i], g))
                              for g in range(ng)], axis=0)
        if i == npg - 1:
            sp = sp + bias_ref[0] * is_last.astype(F32)
        s_pages.append(sp)
    ckv = [c_refs[i][...].astype(BF16) for i in range(npg)]
    s_lat = jnp.concatenate(
        [_dot_t(ql, ckv[i]) + jnp.dot(qp[:, :rope], p_refs[i][...].astype(BF16),
                                      preferred_element_type=F32)
         for i in range(npg)], axis=1) * scale
    diff_update(jnp.concatenate(s_pages, axis=1),
                lambda g: [(i * page, group_rows(v_refs[i], g)) for i in range(npg)])
    mla_update(s_lat, [(i * page, ckv[i]) for i in range(npg)])

    @pl.when(is_last)
    def _():
        kn = kn_ref[0]
        vn = vn_ref[0]
        sn = jnp.concatenate([_dot_t(q[g * rg:(g + 1) * rg], kn[:, g * LANES:(g + 1) * LANES])
                              for g in range(ng)], axis=0)
        diff_update(sn + bias_ref[1], lambda g: [(0, vn[:, g * LANES:(g + 1) * LANES])])
        cn = cn_ref[0]
        sm = (_dot_t(ql, cn[:, :lat_w]) + _dot_t(qp, cn[:, lat_w:])) * scale
        row = lax.broadcasted_iota(jnp.int32, sm.shape, 0)
        col = lax.broadcasted_iota(jnp.int32, sm.shape, 1)
        sm = jnp.where(col <= jnp.bitwise_and(row, dec_seq - 1), sm, NEG_INF)
        mla_update(sm, [(0, cn[:, :lat_w])])

        lam = _lam(lp_ref, lam_init)
        o = accd_ref[...] / ld_ref[0]
        hr = rg // 2
        for g in range(ng):
            od = o[g * rg:g * rg + hr] - lam * o[g * rg + hr:(g + 1) * rg]
            od_ref[0, g * hr:(g + 1) * hr, :] = _rms(od, sg_ref[...]) * (1.0 - lam_init)
        ol_ref[0] = (accl_ref[...] / ll_ref[0][:, :1]).astype(ol_ref.dtype)

    @pl.when(step == total - 1)
    def _():
        for cp in page_copies(nxt, 1 - slot):
            cp.wait()


def _sample_attention(page_table, q_bd, k_new, v_new, bias_s, lam_params, subln_gain, q_mla,
                      kcat_new, cache_k, cache_v, cache_ckv, cache_kpe_t, lam_init):
    nseq, n_pages = page_table.shape
    npg = min(SAMPLE_PAGES_PER_STEP, n_pages)
    assert n_pages % npg == 0
    rows = q_bd.shape[1]
    mrows = q_mla.shape[1]
    dec_seq = mrows // MLA_HEADS
    assert dec_seq & (dec_seq - 1) == 0
    lat_w = cache_ckv.shape[1]
    page = k_new.shape[1]
    rope = cache_kpe_t.shape[0] // (cache_ckv.shape[0] // page)
    scale = (MLA_NOPE_DIM + MLA_ROPE_DIM) ** -0.5

    seq3 = lambda a: pl.BlockSpec((1,) + a.shape[1:], lambda b, c, pt: (b, 0, 0))
    const = lambda a: pl.BlockSpec(a.shape, lambda b, c, pt: (0,) * a.ndim)
    hbm = pl.BlockSpec(memory_space=pl.ANY)
    stat = lambda r: pltpu.VMEM((1, r, LANES), F32)
    pagebuf = lambda rows_, width: pltpu.VMEM((2, npg, rows_, width), F32)
    grid_spec = pltpu.PrefetchScalarGridSpec(
        num_scalar_prefetch=1,
        grid=(nseq, n_pages // npg),
        in_specs=[seq3(q_bd), seq3(k_new), seq3(v_new), const(bias_s), const(lam_params),
                  const(subln_gain), seq3(q_mla), seq3(kcat_new), hbm, hbm, hbm, hbm],
        out_specs=[pl.BlockSpec((1, rows // 2, LANES), lambda b, c, pt: (b, 0, 0)),
                   pl.BlockSpec((1, mrows, lat_w), lambda b, c, pt: (b, 0, 0))],
        scratch_shapes=[pagebuf(page * DIFF_KV_HEADS, LANES), pagebuf(page * DIFF_KV_HEADS, LANES),
                        pagebuf(page, lat_w), pagebuf(rope, page), pltpu.SemaphoreType.DMA((2, 4)),
                        stat(rows), stat(rows), pltpu.VMEM((rows, LANES), F32),
                        stat(mrows), stat(mrows), pltpu.VMEM((mrows, lat_w), F32)])
    return pl.pallas_call(
        functools.partial(_sample_kernel, npg=npg, lam_init=lam_init, scale=scale,
                          dec_seq=dec_seq),
        grid_spec=grid_spec,
        out_shape=[jax.ShapeDtypeStruct((nseq, rows // 2, LANES), F32),
                   jax.ShapeDtypeStruct((nseq, mrows, lat_w), BF16)],
        compiler_params=_cparams(("arbitrary", "arbitrary"), 48 * 1024 * 1024),
        name="sample_attention",
    )(page_table, q_bd, k_new, v_new, bias_s, lam_params, subln_gain, q_mla, kcat_new,
      cache_k, cache_v, cache_ckv, cache_kpe_t)


def _merge_kernel(x_ref, sh_ref, sc_ref, gt_ref, gain_ref, wg_ref, od_ref, ol_ref, wuv_ref,
                  wo_ref, fn_ref, y_ref):
    nb, tt, d = x_ref.shape
    tm = nb * tt
    h = _modulated(x_ref, gain_ref, sc_ref, sh_ref)
    g = jnp.dot(h, wg_ref[...], preferred_element_type=F32)
    dw = od_ref.shape[1]
    ga, gb = g[:, :dw], g[:, dw:]
    ua = (od_ref[...] * (ga * jax.nn.sigmoid(ga))).astype(BF16)
    lat_w = wuv_ref.shape[1]
    vd = wuv_ref.shape[2]
    om = jnp.concatenate(
        [jnp.dot(ol_ref[:, hh * lat_w:(hh + 1) * lat_w], wuv_ref[hh], preferred_element_type=F32)
         for hh in range(MLA_HEADS)], axis=1)
    ub = (om * (gb * jax.nn.sigmoid(gb))).astype(BF16)
    y = (jnp.dot(ua, wo_ref[:dw, :], preferred_element_type=F32)
         + jnp.dot(ub, wo_ref[dw:, :], preferred_element_type=F32))
    xn = x_ref[...] + gt_ref[...] * y.reshape(nb, tt, d)
    y_ref[...] = _rms(xn, fn_ref[...])


def _merge(x3, shift, scale, gate, gain, o_d, o_lat, wts, final_norm):
    nseq, tt, d = x3.shape
    nb, tmt = _row_tiling(nseq, tt)
    tm = nb * tmt
    nt = tt // tmt

    def rowblk(width):
        return pl.BlockSpec((tm, width), lambda i, j: (i * nt + j, 0))

    def seqblk():
        return pl.BlockSpec((nb, 1, d), lambda i, j: (i, 0, 0))

    xblk = pl.BlockSpec((nb, tmt, d), lambda i, j: (i, j, 0))
    return pl.pallas_call(
        _merge_kernel,
        grid=(nseq // nb, nt),
        in_specs=[xblk, seqblk(), seqblk(), seqblk(), _resident((1, d)),
                  _resident(wts["w_gate"].shape), rowblk(o_d.shape[1]), rowblk(o_lat.shape[1]),
                  _resident(wts["w_uv"].shape), _resident(wts["w_out"].shape), _resident((1, d))],
        out_specs=xblk,
        out_shape=jax.ShapeDtypeStruct(x3.shape, F32),
        compiler_params=_cparams(("arbitrary", "arbitrary"), VMEM_LIMIT),
        name="merge",
    )(x3, shift, scale, gate, gain, wts["w_gate"], o_d, o_lat, wts["w_uv"], wts["w_out"],
      final_norm)


def kernel(x_prompt, x_sample, c_prompt, c_sample, cache_diff_k, cache_diff_v, cache_mla_ckv,
           cache_mla_kpe, page_table, w_mod, b_mod, norm_gain, w_in, lambda_params, subln_gain,
           q_a_norm, kv_a_norm, w_qb, w_kvb, w_out, rel_bias, final_norm):
    depth = w_mod.shape[0]
    assert depth == 1, "single-layer trunk"
    batch, seq, d = x_prompt.shape
    nseq, dec_seq, _ = x_sample.shape
    n_pool, page = cache_mla_ckv.shape[1:3]
    past_len = page_table.shape[1] * page
    lam_init = 0.8 - 0.6 * math.exp(-0.3 * 0)
    ng, dv = cache_diff_k.shape[3:]
    lat_w = cache_mla_ckv.shape[3]
    rope = cache_mla_kpe.shape[3]

    wts = _prepare_weights(w_in[0], q_a_norm[0], kv_a_norm[0], w_qb[0], w_kvb[0], w_out[0])
    gain = norm_gain[0][None]
    fnorm = final_norm[None]
    lam_params = lambda_params[0]
    sgain = subln_gain[0][None]

    mod = _modulation(jnp.concatenate([c_prompt, c_sample], axis=0), w_mod[0], b_mod)
    shift, scale, gate = (mod[:, i * d:(i + 1) * d][:, None, :] for i in range(3))
    bias_p, bias_s = _bias_tiles(rel_bias, dec_seq)

    tabs_p = _rope_tables(jnp.arange(seq, dtype=jnp.int32))
    (qd, kd32, vd32, kd16, vd16, qcat, ckv32, kpe32, kcat) = _projection(
        x_prompt, shift[:batch], scale[:batch], gain, wts, tabs_p)
    o_d = _prompt_diff_attention(qd, kd16, vd16, bias_p, lam_params, sgain, batch, seq, lam_init)
    o_lat = _prompt_mla_attention(qcat, kcat, batch, seq)
    y_prompt = _merge(x_prompt, shift[:batch], scale[:batch], gate[:batch], gain, o_d, o_lat,
                      wts, fnorm)
    new_p = (kd32.reshape(1, batch, seq, ng, dv), vd32.reshape(1, batch, seq, ng, dv),
             ckv32.reshape(1, batch, seq, lat_w), kpe32.reshape(1, batch, seq, rope))

    tabs_s = _rope_tables(past_len + jnp.arange(dec_seq, dtype=jnp.int32))
    (qd, kd32, vd32, kd16, vd16, qcat, ckv32, kpe32, kcat) = _projection(
        x_sample, shift[batch:], scale[batch:], gain, wts, tabs_s)
    hd = dv // 2
    q6 = qd.reshape(nseq, dec_seq, ng, DIFF_GROUP, 2, hd)
    q6 = jnp.transpose(q6, (0, 2, 4, 3, 1, 5))
    zq = jnp.zeros_like(q6[:, :, 0])
    q_bd = jnp.stack([jnp.concatenate([q6[:, :, 0], zq], -1),
                      jnp.concatenate([zq, q6[:, :, 1]], -1)], axis=2)
    q_bd = q_bd.reshape(nseq, ng * 2 * DIFF_GROUP * dec_seq, dv)

    def as_page(a):
        a = a.reshape(nseq, dec_seq, -1)
        return jnp.pad(a, ((0, 0), (0, page - dec_seq), (0, 0)))

    q4 = jnp.transpose(qcat.reshape(nseq, dec_seq, MLA_HEADS, MLA_PAD), (0, 2, 1, 3))
    q4 = q4.reshape(nseq, MLA_HEADS * dec_seq, MLA_PAD)
    o_ds, o_ls = _sample_attention(
        page_table, q_bd, as_page(kd16), as_page(vd16), bias_s, lam_params, sgain, q4,
        as_page(kcat), cache_diff_k.reshape(n_pool * page * ng, dv),
        cache_diff_v.reshape(n_pool * page * ng, dv), cache_mla_ckv.reshape(n_pool * page, lat_w),
        jnp.swapaxes(cache_mla_kpe[0], 1, 2).reshape(n_pool * rope, page), lam_init)
    o_ds = jnp.transpose(o_ds.reshape(nseq, ng, DIFF_GROUP, dec_seq, dv), (0, 3, 1, 2, 4))
    o_ds = o_ds.reshape(nseq * dec_seq, ng * DIFF_GROUP * dv)
    o_ls = jnp.transpose(o_ls.reshape(nseq, MLA_HEADS, dec_seq, lat_w), (0, 2, 1, 3))
    o_ls = o_ls.reshape(nseq * dec_seq, MLA_HEADS * lat_w)
    y_sample = _merge(x_sample, shift[batch:], scale[batch:], gate[batch:], gain, o_ds, o_ls,
                      wts, fnorm)
    new_s = (kd32.reshape(1, nseq, dec_seq, ng, dv), vd32.reshape(1, nseq, dec_seq, ng, dv),
             ckv32.reshape(1, nseq, dec_seq, lat_w), kpe32.reshape(1, nseq, dec_seq, rope))
    return (y_prompt, y_sample) + new_p + new_s
```

```python
import functools
import math

import numpy as np
import jax
import jax.numpy as jnp
from jax import lax
from jax.experimental import pallas as pl
from jax.experimental.pallas import tpu as pltpu

F32 = jnp.float32
BF16 = jnp.bfloat16

DIFF_HEADS = 8
DIFF_KV_HEADS = 4
DIFF_GROUP = DIFF_HEADS // DIFF_KV_HEADS
MLA_HEADS = 8
MLA_NOPE_DIM = 128
MLA_ROPE_DIM = 64
ROPE_THETA = 10000.0
REL_BUCKETS = 32
REL_MAX_EXACT = REL_BUCKETS // 2
REL_MAX_DIST = 128
EPS = 1e-6
NEG_INF = -1e30

LANES = 128
V7X_VMEM_BYTES = 64 * 1024 * 1024
VMEM_LIMIT = 56 * 1024 * 1024
ROW_TILE = 256
ATT_TILE = 512
MLA_PAD = 3 * LANES
SAMPLE_PAGES_PER_STEP = 16


def _cparams(sem, vmem=None):
    return pltpu.CompilerParams(dimension_semantics=sem, vmem_limit_bytes=vmem)


def _resident(shape):
    nd = len(shape)
    return pl.BlockSpec(shape, lambda *_: (0,) * nd, pipeline_mode=pl.Buffered(1))


def _rel_bucket(n):
    n = np.asarray(n, np.int64)
    nf = np.maximum(n, 1).astype(np.float32)
    large = REL_MAX_EXACT + (
        np.log(nf / np.float32(REL_MAX_EXACT)) / np.float32(math.log(REL_MAX_DIST / REL_MAX_EXACT))
        * np.float32(REL_BUCKETS - REL_MAX_EXACT)).astype(np.int32)
    large = np.minimum(large, REL_BUCKETS - 1)
    return np.where(n < REL_MAX_EXACT, n, large).astype(np.int32)


def _mod_kernel(c_ref, w_ref, b_ref, o_ref):
    c = c_ref[...]
    a = (c * jax.nn.sigmoid(c)).astype(BF16)
    o_ref[...] = jnp.dot(a, w_ref[...].astype(BF16), preferred_element_type=F32) + b_ref[...]


def _modulation(c, w_mod, b_mod):
    rows, d = c.shape
    n = w_mod.shape[1]
    tn = 1536
    return pl.pallas_call(
        _mod_kernel,
        grid=(n // tn,),
        in_specs=[pl.BlockSpec((rows, d), lambda j: (0, 0)),
                  pl.BlockSpec((d, tn), lambda j: (0, j)),
                  pl.BlockSpec((1, tn), lambda j: (0, j))],
        out_specs=pl.BlockSpec((rows, tn), lambda j: (0, j)),
        out_shape=jax.ShapeDtypeStruct((rows, n), F32),
        compiler_params=_cparams(("arbitrary",), 48 * 1024 * 1024),
        name="modulation",
    )(c, w_mod, b_mod)


def _bias_kernel(tab_ref, bp_ref, bs_ref, hs_ref, op_ref, os_ref):
    far = REL_BUCKETS - 1
    for h in range(DIFF_HEADS):
        for v in range(2):
            bkt = bp_ref[v]
            acc = jnp.zeros(bkt.shape, F32)
            for b in range(far):
                acc = jnp.where(bkt == b, tab_ref[b, h] - tab_ref[far, h], acc)
            op_ref[h, v] = acc
    head = hs_ref[...]
    for v in range(2):
        bkt = bs_ref[v]
        acc = jnp.zeros(bkt.shape, F32)
        for b in range(far):
            val = jnp.zeros(bkt.shape, F32)
            for h in range(DIFF_HEADS):
                val = jnp.where(head == h, tab_ref[b, h] - tab_ref[far, h], val)
            acc = jnp.where(bkt == b, val, acc)
        os_ref[v] = jnp.where(bkt == REL_BUCKETS, NEG_INF, acc)


def _bias_tiles(rel_bias, dec_seq):
    i = np.arange(LANES)[:, None]
    j = np.arange(LANES)[None, :]
    bp = np.stack([_rel_bucket(np.maximum(i - j, 0)), _rel_bucket(LANES + i - j)])
    t = i % dec_seq
    head = ((i // (4 * dec_seq)) * DIFF_GROUP + (i // dec_seq) % DIFF_GROUP) + 0 * j
    last = _rel_bucket(LANES + t - j)
    new = np.where(j <= t, _rel_bucket(np.maximum(t - j, 0)), REL_BUCKETS)
    bs = np.stack([last, new])
    vm = pl.BlockSpec(memory_space=pltpu.VMEM)
    return pl.pallas_call(
        _bias_kernel,
        in_specs=[pl.BlockSpec(memory_space=pltpu.SMEM), vm, vm, vm],
        out_specs=[vm, vm],
        out_shape=[jax.ShapeDtypeStruct((DIFF_HEADS, 2, LANES, LANES), F32),
                   jax.ShapeDtypeStruct((2, LANES, LANES), F32)],
        name="rel_bias_tiles",
    )(rel_bias, jnp.asarray(bp, jnp.int32), jnp.asarray(bs, jnp.int32),
      jnp.asarray(head, jnp.int32))


def _rms(x, gain):
    return x * lax.rsqrt(jnp.mean(x * x, axis=-1, keepdims=True) + EPS) * gain


def _modulated(x_ref, gain_ref, sc_ref, sh_ref):
    nb, tt, d = x_ref.shape
    h = _rms(x_ref[...], gain_ref[...]) * (1.0 + sc_ref[...]) + sh_ref[...]
    return h.reshape(nb * tt, d).astype(BF16)


def _proj_kernel(x_ref, sh_ref, sc_ref, gain_ref, w_ref, qag_ref, wqb_ref, wuk_ref, kvg_ref,
                 ck_ref, sk_ref,
                 qd_ref, kd32_ref, vd32_ref, kd16_ref, vd16_ref, qcat_ref, ckv32_ref, kpe32_ref,
                 kcat_ref, *, cols):
    nb, tt, _ = x_ref.shape
    tm = nb * tt
    h = _modulated(x_ref, gain_ref, sc_ref, sh_ref)

    def seg(name):
        a, b = cols[name]
        return jnp.dot(h, w_ref[:, a:b], preferred_element_type=F32)

    qd_ref[...] = (seg("q_d") * 0.125).astype(BF16)
    kd = seg("k_d")
    vd = seg("v_d")
    for g in range(DIFF_KV_HEADS):
        kd32_ref[pl.ds(g, tm, stride=DIFF_KV_HEADS), :] = kd[:, g * LANES:(g + 1) * LANES]
        vd32_ref[pl.ds(g, tm, stride=DIFF_KV_HEADS), :] = vd[:, g * LANES:(g + 1) * LANES]
    kd16_ref[...] = kd.astype(BF16)
    vd16_ref[...] = vd.astype(BF16)

    qn = _rms(seg("q_a"), qag_ref[...]).astype(BF16)
    nope_w = MLA_HEADS * MLA_NOPE_DIM
    pe_w = MLA_HEADS * LANES
    qall = jnp.dot(qn, wqb_ref[...], preferred_element_type=F32)
    pe = qall[:, nope_w:nope_w + pe_w].reshape(nb, tt, pe_w)
    pe_sw = qall[:, nope_w + pe_w:].reshape(nb, tt, pe_w)
    cq = jnp.concatenate([ck_ref[...]] * MLA_HEADS, axis=-1)
    sq = jnp.concatenate([sk_ref[...]] * MLA_HEADS, axis=-1)
    qpe = (pe * cq + pe_sw * sq).reshape(tm, pe_w).astype(BF16)
    lat_w = MLA_PAD - LANES
    for hh in range(MLA_HEADS):
        nope = qall[:, hh * MLA_NOPE_DIM:(hh + 1) * MLA_NOPE_DIM].astype(BF16)
        ql = jnp.dot(nope, wuk_ref[hh], preferred_element_type=F32)
        qcat_ref[:, hh * MLA_PAD:hh * MLA_PAD + lat_w] = ql.astype(BF16)
        qcat_ref[:, hh * MLA_PAD + lat_w:(hh + 1) * MLA_PAD] = qpe[:, hh * LANES:(hh + 1) * LANES]

    ckv = _rms(seg("kv"), kvg_ref[...])
    ckv32_ref[...] = ckv
    kcat_ref[:, :lat_w] = ckv.astype(BF16)
    kp = (seg("kpe").reshape(nb, tt, LANES) * ck_ref[...]
          + seg("kpe_sw").reshape(nb, tt, LANES) * sk_ref[...]).reshape(tm, LANES)
    kpe32_ref[...] = kp[:, :MLA_ROPE_DIM]
    kcat_ref[:, lat_w:] = kp.astype(BF16)


def _row_tiling(nseq, tt):
    if tt >= ROW_TILE:
        assert tt % ROW_TILE == 0
        return 1, ROW_TILE
    nb = min(ROW_TILE // tt, nseq)
    assert nseq % nb == 0
    return nb, tt


def _projection(x3, shift, scale, gain, wts, tabs):
    nseq, tt, d = x3.shape
    nb, tmt = _row_tiling(nseq, tt)
    tm = nb * tmt
    rows = nseq * tt
    nt = tt // tmt
    grid = (nseq // nb, nt)

    def rowblk(width):
        return pl.BlockSpec((tm, width), lambda i, j: (i * nt + j, 0))

    def seqblk():
        return pl.BlockSpec((nb, 1, d), lambda i, j: (i, 0, 0))

    def tabblk(width):
        return pl.BlockSpec((1, tmt, width), lambda i, j: (0, j, 0))

    diff_q = wts["cols"]["q_d"][1] - wts["cols"]["q_d"][0]
    diff_kv = wts["cols"]["k_d"][1] - wts["cols"]["k_d"][0]
    lat_w = MLA_PAD - LANES
    pe_w = MLA_HEADS * LANES
    out_shapes = [
        ((rows, diff_q), BF16), ((rows * DIFF_KV_HEADS, LANES), F32),
        ((rows * DIFF_KV_HEADS, LANES), F32),
        ((rows, diff_kv), BF16), ((rows, diff_kv), BF16), ((rows, MLA_HEADS * MLA_PAD), BF16),
        ((rows, lat_w), F32), ((rows, MLA_ROPE_DIM), F32), ((rows, MLA_PAD), BF16)]
    return pl.pallas_call(
        functools.partial(_proj_kernel, cols=wts["cols"]),
        grid=grid,
        in_specs=[pl.BlockSpec((nb, tmt, d), lambda i, j: (i, j, 0)), seqblk(), seqblk(),
                  _resident((1, d)), _resident(wts["w_in"].shape), _resident((1, wts["qa_gain"].shape[1])),
                  _resident(wts["w_qb"].shape), _resident(wts["w_uk"].shape),
                  _resident((1, lat_w)),
                  tabblk(LANES), tabblk(LANES)],
        out_specs=[pl.BlockSpec((tm * s[0] // rows, s[1]), lambda i, j: (i * nt + j, 0))
                   for s, _ in out_shapes],
        out_shape=[jax.ShapeDtypeStruct(s, dt) for s, dt in out_shapes],
        compiler_params=_cparams(("arbitrary", "arbitrary"), VMEM_LIMIT),
        name="projection",
    )(x3, shift, scale, gain, wts["w_in"], wts["qa_gain"], wts["w_qb"], wts["w_uk"],
      wts["kv_gain"], tabs["ck"], tabs["sk"])


def _rope_tables(pos):
    r = MLA_ROPE_DIM
    freqs = ROPE_THETA ** (-jnp.arange(0, r, 2, dtype=F32) / r)
    ang = pos.astype(F32)[:, None] * freqs[None, :]
    cos, sin = jnp.cos(ang), jnp.sin(ang)
    zeros = jnp.zeros((pos.shape[0], LANES - r), F32)
    ck = jnp.concatenate([cos, cos, zeros], axis=-1)
    sk = jnp.concatenate([-sin, sin, zeros], axis=-1)
    return dict(ck=ck[None], sk=sk[None])


def _prepare_weights(w_in, q_a_norm, kv_a_norm, w_qb, w_kvb, w_out):
    d = w_in.shape[0]
    q_rank, kv_rank = w_qb.shape[0], w_kvb.shape[0]
    diff_w = d // 2
    diff_kv = diff_w // DIFF_GROUP
    r = MLA_ROPE_DIM
    widths = [("q_d", diff_w), ("k_d", diff_kv), ("v_d", diff_kv), ("g_a", diff_w),
              ("q_a", q_rank), ("kv", kv_rank), ("kpe", r), ("g_b", d - diff_w)]
    src, a = {}, 0
    for name, w in widths:
        src[name] = (a, a + w)
        a += w
    assert a == w_in.shape[1]

    def cut(name):
        return w_in[:, src[name][0]:src[name][1]]

    swap = jnp.concatenate([jnp.arange(r // 2, r), jnp.arange(0, r // 2)])
    zpad = jnp.zeros((d, LANES - r), w_in.dtype)
    kpe = cut("kpe")
    parts = [("q_d", cut("q_d")), ("k_d", cut("k_d")), ("v_d", cut("v_d")), ("q_a", cut("q_a")),
             ("kv", cut("kv")), ("kpe", jnp.concatenate([kpe, zpad], 1)),
             ("kpe_sw", jnp.concatenate([kpe[:, swap], zpad], 1))]
    cols, a = {}, 0
    for name, p in parts:
        cols[name] = (a, a + p.shape[1])
        a += p.shape[1]
    w_proj = jnp.concatenate([p for _, p in parts], axis=1).astype(BF16)
    w_gate = jnp.concatenate([cut("g_a"), cut("g_b")], axis=1).astype(BF16)

    nope = w_qb[:, :, :MLA_NOPE_DIM].reshape(q_rank, MLA_HEADS * MLA_NOPE_DIM)
    pe = w_qb[:, :, MLA_NOPE_DIM:]
    zq = jnp.zeros((q_rank, MLA_HEADS, LANES - r), w_qb.dtype)
    pe_p = jnp.concatenate([pe, zq], axis=-1).reshape(q_rank, MLA_HEADS * LANES)
    pe_sw = jnp.concatenate([pe[:, :, swap], zq], axis=-1).reshape(q_rank, MLA_HEADS * LANES)
    w_qb_aug = jnp.concatenate([nope, pe_p, pe_sw], axis=1).astype(BF16)
    w_uk = jnp.transpose(w_kvb[:, :, :MLA_NOPE_DIM], (1, 2, 0)).astype(BF16)
    w_uv = jnp.transpose(w_kvb[:, :, MLA_NOPE_DIM:], (1, 0, 2)).astype(BF16)
    return dict(cols=cols, w_in=w_proj, w_gate=w_gate, w_qb=w_qb_aug, w_uk=w_uk, w_uv=w_uv,
                qa_gain=q_a_norm[None], kv_gain=kv_a_norm[None], w_out=w_out.astype(BF16))


def _softmax_step(s, m_ref, l_ref, idx):
    m_prev = m_ref[idx]
    m_new = jnp.maximum(m_prev, jnp.max(s, axis=-1, keepdims=True))
    alpha = jnp.exp(m_prev - m_new)
    p = jnp.exp(s - m_new[:, :1])
    l_ref[idx] = alpha * l_ref[idx] + jnp.sum(p, axis=-1, keepdims=True)
    m_ref[idx] = m_new
    return p, alpha


def _lam(lp_ref, lam_init):
    lp = lp_ref[...]
    a = jnp.sum(lp[0:1] * lp[1:2], axis=-1, keepdims=True)
    b = jnp.sum(lp[2:3] * lp[3:4], axis=-1, keepdims=True)
    return jnp.exp(a) - jnp.exp(b) + lam_init


def _causal_tiles(n):
    pairs = [(i, j) for i in range(n) for j in range(i + 1)]
    return (jnp.asarray([p[0] for p in pairs], jnp.int32),
            jnp.asarray([p[1] for p in pairs], jnp.int32))


def _dot_t(a, b):
    return lax.dot_general(a, b, (((1,), (1,)), ((), ())), preferred_element_type=F32)


def _pdiff_kernel(qi_ref, ki_ref, q_ref, k_ref, v_ref, bias_ref, lp_ref, sg_ref, o_ref, m_ref, l_ref,
                  acc_ref, *, lam_init):
    qi, ki = qi_ref[pl.program_id(2)], ki_ref[pl.program_id(2)]
    tq, tk = q_ref.shape[0], k_ref.shape[0]
    nblk = tq // LANES

    @pl.when(ki == 0)
    def _():
        m_ref[...] = jnp.full(m_ref.shape, NEG_INF, F32)
        l_ref[...] = jnp.zeros(l_ref.shape, F32)
        acc_ref[...] = jnp.zeros(acc_ref.shape, F32)

    def bias_tile(r, diag):
        z = jnp.zeros((LANES, LANES), F32)
        same, prev = bias_ref[r, 0], bias_ref[r, 1]
        rows = []
        for i in range(nblk):
            if diag:
                blocks = [same if j == i else prev if j == i - 1 else z for j in range(nblk)]
            else:
                blocks = [prev if (i == 0 and j == nblk - 1) else z for j in range(nblk)]
            rows.append(jnp.concatenate(blocks, axis=1))
        return jnp.concatenate(rows, axis=0)

    def step(mode):
        q = q_ref[...]
        k = k_ref[...]
        v = v_ref[...]
        lane = lax.broadcasted_iota(jnp.int32, k.shape, 1)
        half = k.shape[1] // 2
        kmap = (jnp.where(lane < half, k, jnp.zeros_like(k)),
                jnp.where(lane >= half, k, jnp.zeros_like(k)))
        if mode == "diag":
            row = lax.broadcasted_iota(jnp.int32, (tq, tk), 0)
            col = lax.broadcasted_iota(jnp.int32, (tq, tk), 1)
            causal = col <= row
        for r in range(DIFF_GROUP):
            qr = q[:, r * LANES:(r + 1) * LANES]
            if mode != "full":
                bias = bias_tile(r, mode == "diag")
            for mp in range(2):
                s = _dot_t(qr, kmap[mp])
                if mode != "full":
                    s = s + bias
                if mode == "diag":
                    s = jnp.where(causal, s, NEG_INF)
                idx = r * 2 + mp
                p, alpha = _softmax_step(s, m_ref, l_ref, idx)
                acc_ref[idx] = alpha * acc_ref[idx] + jnp.dot(
                    p.astype(BF16), v, preferred_element_type=F32)

    @pl.when(ki < qi - 1)
    def _():
        step("full")

    @pl.when(ki == qi - 1)
    def _():
        step("prev")

    @pl.when(ki == qi)
    def _():
        step("diag")
        lam = _lam(lp_ref, lam_init)
        for r in range(DIFF_GROUP):
            o = (acc_ref[2 * r] / l_ref[2 * r] - lam * (acc_ref[2 * r + 1] / l_ref[2 * r + 1]))
            o_ref[:, r * LANES:(r + 1) * LANES] = _rms(o, sg_ref[...]) * (1.0 - lam_init)


def _prompt_diff_attention(qd, kd, vd, bias_p, lam_params, subln_gain, batch, seq, lam_init):
    t = ATT_TILE
    assert seq % t == 0 and t % LANES == 0
    n = seq // t
    gw = DIFF_GROUP * LANES
    qi_tab, ki_tab = _causal_tiles(n)
    q_spec = pl.BlockSpec((t, gw), lambda b, g, s, qt, kt: (b * n + qt[s], g))
    kv_spec = pl.BlockSpec((t, LANES), lambda b, g, s, qt, kt: (b * n + kt[s], g))
    const = lambda a: pl.BlockSpec(a.shape, lambda b, g, s, qt, kt: (0,) * a.ndim)
    grid_spec = pltpu.PrefetchScalarGridSpec(
        num_scalar_prefetch=2,
        grid=(batch, DIFF_KV_HEADS, qi_tab.shape[0]),
        in_specs=[q_spec, kv_spec, kv_spec,
                  pl.BlockSpec((DIFF_GROUP, 2, LANES, LANES), lambda b, g, s, qt, kt: (g, 0, 0, 0)),
                  const(lam_params), const(subln_gain)],
        out_specs=q_spec,
        scratch_shapes=[pltpu.VMEM((2 * DIFF_GROUP, t, LANES), F32)] * 3)
    return pl.pallas_call(
        functools.partial(_pdiff_kernel, lam_init=lam_init),
        grid_spec=grid_spec,
        out_shape=jax.ShapeDtypeStruct((batch * seq, DIFF_KV_HEADS * gw), F32),
        compiler_params=_cparams(("arbitrary",) * 3, 48 * 1024 * 1024),
        name="prompt_diff_attention",
    )(qi_tab, ki_tab, qd, kd, vd, bias_p, lam_params, subln_gain)


def _pmla_kernel(qi_ref, ki_ref, q_ref, k_ref, o_ref, m_ref, l_ref, acc_ref, *, scale):
    qi, ki = qi_ref[pl.program_id(1)], ki_ref[pl.program_id(1)]
    tq, tk = q_ref.shape[0], k_ref.shape[0]
    lat_w = MLA_PAD - LANES

    @pl.when(ki == 0)
    def _():
        m_ref[...] = jnp.full(m_ref.shape, NEG_INF, F32)
        l_ref[...] = jnp.zeros(l_ref.shape, F32)
        acc_ref[...] = jnp.zeros(acc_ref.shape, F32)

    def step(diag):
        k = k_ref[...]
        ckv = k[:, :lat_w]
        if diag:
            row = lax.broadcasted_iota(jnp.int32, (tq, tk), 0)
            col = lax.broadcasted_iota(jnp.int32, (tq, tk), 1)
            causal = col <= row
        for h in range(MLA_HEADS):
            s = _dot_t(q_ref[:, h * MLA_PAD:(h + 1) * MLA_PAD], k) * scale
            if diag:
                s = jnp.where(causal, s, NEG_INF)
            p, alpha = _softmax_step(s, m_ref, l_ref, h)
            acc_ref[h] = alpha[:, :1] * acc_ref[h] + jnp.dot(
                p.astype(BF16), ckv, preferred_element_type=F32)

    @pl.when(ki < qi)
    def _():
        step(False)

    @pl.when(ki == qi)
    def _():
        step(True)
        for h in range(MLA_HEADS):
            o_ref[:, h * lat_w:(h + 1) * lat_w] = (acc_ref[h] / l_ref[h][:, :1]).astype(o_ref.dtype)


def _prompt_mla_attention(qcat, kcat, batch, seq):
    t = ATT_TILE
    n = seq // t
    lat_w = MLA_PAD - LANES
    scale = (MLA_NOPE_DIM + MLA_ROPE_DIM) ** -0.5
    qi_tab, ki_tab = _causal_tiles(n)
    grid_spec = pltpu.PrefetchScalarGridSpec(
        num_scalar_prefetch=2,
        grid=(batch, qi_tab.shape[0]),
        in_specs=[pl.BlockSpec((t, MLA_HEADS * MLA_PAD), lambda b, s, qt, kt: (b * n + qt[s], 0)),
                  pl.BlockSpec((t, MLA_PAD), lambda b, s, qt, kt: (b * n + kt[s], 0))],
        out_specs=pl.BlockSpec((t, MLA_HEADS * lat_w), lambda b, s, qt, kt: (b * n + qt[s], 0)),
        scratch_shapes=[pltpu.VMEM((MLA_HEADS, t, LANES), F32), pltpu.VMEM((MLA_HEADS, t, LANES), F32),
                        pltpu.VMEM((MLA_HEADS, t, lat_w), F32)])
    return pl.pallas_call(
        functools.partial(_pmla_kernel, scale=scale),
        grid_spec=grid_spec,
        out_shape=jax.ShapeDtypeStruct((batch * seq, MLA_HEADS * lat_w), BF16),
        compiler_params=_cparams(("arbitrary",) * 2, 48 * 1024 * 1024),
        name="prompt_mla_attention",
    )(qi_tab, ki_tab, qcat, kcat)


def _sample_kernel(pt_ref, q_ref, kn_ref, vn_ref, bias_ref, lp_ref, sg_ref, qm_ref, cn_ref,
                   k_hbm, v_hbm, c_hbm, p_hbm, od_ref, ol_ref, kbuf, vbuf, cbuf, pbuf, sem,
                   md_ref, ld_ref, accd_ref, ml_ref, ll_ref, accl_ref,
                   *, npg, lam_init, scale, dec_seq):
    b, c = pl.program_id(0), pl.program_id(1)
    nc = pl.num_programs(1)
    total = pl.num_programs(0) * nc
    step = b * nc + c
    nslot = kbuf.shape[0]
    ahead = nslot - 1
    slot = lax.rem(step, nslot)
    is_last = c == nc - 1
    ng = DIFF_KV_HEADS
    rows = q_ref.shape[1]
    rg = rows // ng
    page, lat_w = cbuf.shape[2:]
    rope = pbuf.shape[2]
    streams = ((k_hbm, kbuf), (v_hbm, vbuf), (c_hbm, cbuf), (p_hbm, pbuf))

    def page_copies(step_, slot_):
        bb = lax.div(step_, nc)
        cc = lax.rem(step_, nc)
        out = []
        for i in range(npg):
            pg = pt_ref[bb, cc * npg + i]
            for j, (hbm, buf) in enumerate(streams):
                n = buf.shape[2]
                src = hbm.at[pl.ds(pl.multiple_of(pg * n, n), n), :]
                out.append(pltpu.make_async_copy(src, buf.at[slot_, i], sem.at[slot_, j]))
        return out

    def fetch(offset):
        return page_copies(jnp.minimum(step + offset, total - 1), lax.rem(step + offset, nslot))

    @pl.when(step == 0)
    def _():
        for offset in range(ahead):
            for cp in fetch(offset):
                cp.start()

    @pl.when(c == 0)
    def _():
        for m_ref, l_ref, acc_ref in ((md_ref, ld_ref, accd_ref), (ml_ref, ll_ref, accl_ref)):
            m_ref[...] = jnp.full(m_ref.shape, NEG_INF, F32)
            l_ref[...] = jnp.zeros(l_ref.shape, F32)
            acc_ref[...] = jnp.zeros(acc_ref.shape, F32)

    for cp in fetch(0):
        cp.wait()
    for cp in fetch(ahead):
        cp.start()
    k_refs, v_refs, c_refs, p_refs = ([buf.at[slot, i] for i in range(npg)] for _, buf in streams)

    q = q_ref[0]
    ql = qm_ref[0, :, :lat_w]
    qp = qm_ref[0, :, lat_w:]

    def group_rows(ref, g):
        return ref[pl.ds(g, page, stride=ng), :].astype(BF16)

    def diff_update(s, value_of):
        p, alpha = _softmax_step(s, md_ref, ld_ref, 0)
        p = p.astype(BF16)
        accd_ref[...] = alpha * accd_ref[...]
        for g in range(ng):
            tot = None
            for off, val in value_of(g):
                part = jnp.dot(p[g * rg:(g + 1) * rg, off:off + val.shape[0]], val,
                               preferred_element_type=F32)
                tot = part if tot is None else tot + part
            accd_ref[g * rg:(g + 1) * rg, :] += tot

    def mla_update(s, values):
        p, alpha = _softmax_step(s, ml_ref, ll_ref, 0)
        p = p.astype(BF16)
        tot = None
        for off, val in values:
            part = jnp.dot(p[:, off:off + val.shape[0]], val, preferred_element_type=F32)
            tot = part if tot is None else tot + part
        accl_ref[...] = alpha[:, :1] * accl_ref[...] + tot

    s_pages = []
    for i in range(npg):
        sp = jnp.concatenate([_dot_t(q[g * rg:(g + 1) * rg], group_rows(k_refs[i], g))
                              for g in range(ng)], axis=0)
        if i == npg - 1:
            sp = sp + bias_ref[0] * is_last.astype(F32)
        s_pages.append(sp)
    ckv = [c_refs[i][...].astype(BF16) for i in range(npg)]
    s_lat = jnp.concatenate(
        [_dot_t(ql, ckv[i]) + jnp.dot(qp[:, :rope], p_refs[i][...].astype(BF16),
                                      preferred_element_type=F32)
         for i in range(npg)], axis=1) * scale
    diff_update(jnp.concatenate(s_pages, axis=1),
                lambda g: [(i * page, group_rows(v_refs[i], g)) for i in range(npg)])
    mla_update(s_lat, [(i * page, ckv[i]) for i in range(npg)])

    @pl.when(is_last)
    def _():
        kn = kn_ref[0]
        vn = vn_ref[0]
        sn = jnp.concatenate([_dot_t(q[g * rg:(g + 1) * rg], kn[:, g * LANES:(g + 1) * LANES])
                              for g in range(ng)], axis=0)
        diff_update(sn + bias_ref[1], lambda g: [(0, vn[:, g * LANES:(g + 1) * LANES])])
        cn = cn_ref[0]
        sm = (_dot_t(ql, cn[:, :lat_w]) + _dot_t(qp, cn[:, lat_w:])) * scale
        row = lax.broadcasted_iota(jnp.int32, sm.shape, 0)
        col = lax.broadcasted_iota(jnp.int32, sm.shape, 1)
        sm = jnp.where(col <= jnp.bitwise_and(row, dec_seq - 1), sm, NEG_INF)
        mla_update(sm, [(0, cn[:, :lat_w])])

        lam = _lam(lp_ref, lam_init)
        o = accd_ref[...] / ld_ref[0]
        hr = rg // 2
        for g in range(ng):
            od = o[g * rg:g * rg + hr] - lam * o[g * rg + hr:(g + 1) * rg]
            od_ref[0, g * hr:(g + 1) * hr, :] = _rms(od, sg_ref[...]) * (1.0 - lam_init)
        ol_ref[0] = (accl_ref[...] / ll_ref[0][:, :1]).astype(ol_ref.dtype)

    @pl.when(step == total - 1)
    def _():
        for offset in range(1, nslot):
            for cp in fetch(offset):
                cp.wait()


def _sample_attention(page_table, q_bd, k_new, v_new, bias_s, lam_params, subln_gain, q_mla,
                      kcat_new, cache_k, cache_v, cache_ckv, cache_kpe_t, lam_init):
    nseq, n_pages = page_table.shape
    npg = min(SAMPLE_PAGES_PER_STEP, n_pages)
    assert n_pages % npg == 0
    rows = q_bd.shape[1]
    mrows = q_mla.shape[1]
    dec_seq = mrows // MLA_HEADS
    assert dec_seq & (dec_seq - 1) == 0
    lat_w = cache_ckv.shape[1]
    page = k_new.shape[1]
    rope = cache_kpe_t.shape[0] // (cache_ckv.shape[0] // page)
    scale = (MLA_NOPE_DIM + MLA_ROPE_DIM) ** -0.5

    seq3 = lambda a: pl.BlockSpec((1,) + a.shape[1:], lambda b, c, pt: (b, 0, 0))
    const = lambda a: pl.BlockSpec(a.shape, lambda b, c, pt: (0,) * a.ndim)
    hbm = pl.BlockSpec(memory_space=pl.ANY)
    stat = lambda r: pltpu.VMEM((1, r, LANES), F32)
    nslot = 3
    pagebuf = lambda rows_, width: pltpu.VMEM((nslot, npg, rows_, width), F32)
    grid_spec = pltpu.PrefetchScalarGridSpec(
        num_scalar_prefetch=1,
        grid=(nseq, n_pages // npg),
        in_specs=[seq3(q_bd), seq3(k_new), seq3(v_new), const(bias_s), const(lam_params),
                  const(subln_gain), seq3(q_mla), seq3(kcat_new), hbm, hbm, hbm, hbm],
        out_specs=[pl.BlockSpec((1, rows // 2, LANES), lambda b, c, pt: (b, 0, 0)),
                   pl.BlockSpec((1, mrows, lat_w), lambda b, c, pt: (b, 0, 0))],
        scratch_shapes=[pagebuf(page * DIFF_KV_HEADS, LANES), pagebuf(page * DIFF_KV_HEADS, LANES),
                        pagebuf(page, lat_w), pagebuf(rope, page), pltpu.SemaphoreType.DMA((nslot, 4)),
                        stat(rows), stat(rows), pltpu.VMEM((rows, LANES), F32),
                        stat(mrows), stat(mrows), pltpu.VMEM((mrows, lat_w), F32)])
    return pl.pallas_call(
        functools.partial(_sample_kernel, npg=npg, lam_init=lam_init, scale=scale,
                          dec_seq=dec_seq),
        grid_spec=grid_spec,
        out_shape=[jax.ShapeDtypeStruct((nseq, rows // 2, LANES), F32),
                   jax.ShapeDtypeStruct((nseq, mrows, lat_w), BF16)],
        compiler_params=_cparams(("arbitrary", "arbitrary"), 48 * 1024 * 1024),
        name="sample_attention",
    )(page_table, q_bd, k_new, v_new, bias_s, lam_params, subln_gain, q_mla, kcat_new,
      cache_k, cache_v, cache_ckv, cache_kpe_t)


def _merge_kernel(x_ref, sh_ref, sc_ref, gt_ref, gain_ref, wg_ref, od_ref, ol_ref, wuv_ref,
                  wo_ref, fn_ref, y_ref):
    nb, tt, d = x_ref.shape
    tm = nb * tt
    h = _modulated(x_ref, gain_ref, sc_ref, sh_ref)
    g = jnp.dot(h, wg_ref[...], preferred_element_type=F32)
    dw = od_ref.shape[1]
    ga, gb = g[:, :dw], g[:, dw:]
    ua = (od_ref[...] * (ga * jax.nn.sigmoid(ga))).astype(BF16)
    lat_w = wuv_ref.shape[1]
    vd = wuv_ref.shape[2]
    om = jnp.concatenate(
        [jnp.dot(ol_ref[:, hh * lat_w:(hh + 1) * lat_w], wuv_ref[hh], preferred_element_type=F32)
         for hh in range(MLA_HEADS)], axis=1)
    ub = (om * (gb * jax.nn.sigmoid(gb))).astype(BF16)
    y = (jnp.dot(ua, wo_ref[:dw, :], preferred_element_type=F32)
         + jnp.dot(ub, wo_ref[dw:, :], preferred_element_type=F32))
    xn = x_ref[...] + gt_ref[...] * y.reshape(nb, tt, d)
    y_ref[...] = _rms(xn, fn_ref[...])


def _merge(x3, shift, scale, gate, gain, o_d, o_lat, wts, final_norm):
    nseq, tt, d = x3.shape
    nb, tmt = _row_tiling(nseq, tt)
    tm = nb * tmt
    nt = tt // tmt

    def rowblk(width):
        return pl.BlockSpec((tm, width), lambda i, j: (i * nt + j, 0))

    def seqblk():
        return pl.BlockSpec((nb, 1, d), lambda i, j: (i, 0, 0))

    xblk = pl.BlockSpec((nb, tmt, d), lambda i, j: (i, j, 0))
    return pl.pallas_call(
        _merge_kernel,
        grid=(nseq // nb, nt),
        in_specs=[xblk, seqblk(), seqblk(), seqblk(), _resident((1, d)),
                  _resident(wts["w_gate"].shape), rowblk(o_d.shape[1]), rowblk(o_lat.shape[1]),
                  _resident(wts["w_uv"].shape), _resident(wts["w_out"].shape), _resident((1, d))],
        out_specs=xblk,
        out_shape=jax.ShapeDtypeStruct(x3.shape, F32),
        compiler_params=_cparams(("arbitrary", "arbitrary"), VMEM_LIMIT),
        name="merge",
    )(x3, shift, scale, gate, gain, wts["w_gate"], o_d, o_lat, wts["w_uv"], wts["w_out"],
      final_norm)


def kernel(x_prompt, x_sample, c_prompt, c_sample, cache_diff_k, cache_diff_v, cache_mla_ckv,
           cache_mla_kpe, page_table, w_mod, b_mod, norm_gain, w_in, lambda_params, subln_gain,
           q_a_norm, kv_a_norm, w_qb, w_kvb, w_out, rel_bias, final_norm):
    depth = w_mod.shape[0]
    assert depth == 1, "single-layer trunk"
    batch, seq, d = x_prompt.shape
    nseq, dec_seq, _ = x_sample.shape
    n_pool, page = cache_mla_ckv.shape[1:3]
    past_len = page_table.shape[1] * page
    lam_init = 0.8 - 0.6 * math.exp(-0.3 * 0)
    ng, dv = cache_diff_k.shape[3:]
    lat_w = cache_mla_ckv.shape[3]
    rope = cache_mla_kpe.shape[3]

    wts = _prepare_weights(w_in[0], q_a_norm[0], kv_a_norm[0], w_qb[0], w_kvb[0], w_out[0])
    gain = norm_gain[0][None]
    fnorm = final_norm[None]
    lam_params = lambda_params[0]
    sgain = subln_gain[0][None]

    mod = _modulation(jnp.concatenate([c_prompt, c_sample], axis=0), w_mod[0], b_mod)
    shift, scale, gate = (mod[:, i * d:(i + 1) * d][:, None, :] for i in range(3))
    bias_p, bias_s = _bias_tiles(rel_bias, dec_seq)

    tabs_p = _rope_tables(jnp.arange(seq, dtype=jnp.int32))
    (qd, kd32, vd32, kd16, vd16, qcat, ckv32, kpe32, kcat) = _projection(
        x_prompt, shift[:batch], scale[:batch], gain, wts, tabs_p)
    o_d = _prompt_diff_attention(qd, kd16, vd16, bias_p, lam_params, sgain, batch, seq, lam_init)
    o_lat = _prompt_mla_attention(qcat, kcat, batch, seq)
    y_prompt = _merge(x_prompt, shift[:batch], scale[:batch], gate[:batch], gain, o_d, o_lat,
                      wts, fnorm)
    new_p = (kd32.reshape(1, batch, seq, ng, dv), vd32.reshape(1, batch, seq, ng, dv),
             ckv32.reshape(1, batch, seq, lat_w), kpe32.reshape(1, batch, seq, rope))

    tabs_s = _rope_tables(past_len + jnp.arange(dec_seq, dtype=jnp.int32))
    (qd, kd32, vd32, kd16, vd16, qcat, ckv32, kpe32, kcat) = _projection(
        x_sample, shift[batch:], scale[batch:], gain, wts, tabs_s)
    hd = dv // 2
    q6 = qd.reshape(nseq, dec_seq, ng, DIFF_GROUP, 2, hd)
    q6 = jnp.transpose(q6, (0, 2, 4, 3, 1, 5))
    zq = jnp.zeros_like(q6[:, :, 0])
    q_bd = jnp.stack([jnp.concatenate([q6[:, :, 0], zq], -1),
                      jnp.concatenate([zq, q6[:, :, 1]], -1)], axis=2)
    q_bd = q_bd.reshape(nseq, ng * 2 * DIFF_GROUP * dec_seq, dv)

    def as_page(a):
        a = a.reshape(nseq, dec_seq, -1)
        return jnp.pad(a, ((0, 0), (0, page - dec_seq), (0, 0)))

    q4 = jnp.transpose(qcat.reshape(nseq, dec_seq, MLA_HEADS, MLA_PAD), (0, 2, 1, 3))
    q4 = q4.reshape(nseq, MLA_HEADS * dec_seq, MLA_PAD)
    o_ds, o_ls = _sample_attention(
        page_table, q_bd, as_page(kd16), as_page(vd16), bias_s, lam_params, sgain, q4,
        as_page(kcat), cache_diff_k.reshape(n_pool * page * ng, dv),
        cache_diff_v.reshape(n_pool * page * ng, dv), cache_mla_ckv.reshape(n_pool * page, lat_w),
        jnp.swapaxes(cache_mla_kpe[0], 1, 2).reshape(n_pool * rope, page), lam_init)
    o_ds = jnp.transpose(o_ds.reshape(nseq, ng, DIFF_GROUP, dec_seq, dv), (0, 3, 1, 2, 4))
    o_ds = o_ds.reshape(nseq * dec_seq, ng * DIFF_GROUP * dv)
    o_ls = jnp.transpose(o_ls.reshape(nseq, MLA_HEADS, dec_seq, lat_w), (0, 2, 1, 3))
    o_ls = o_ls.reshape(nseq * dec_seq, MLA_HEADS * lat_w)
    y_sample = _merge(x_sample, shift[batch:], scale[batch:], gate[batch:], gain, o_ds, o_ls,
                      wts, fnorm)
    new_s = (kd32.reshape(1, nseq, dec_seq, ng, dv), vd32.reshape(1, nseq, dec_seq, ng, dv),
             ckv32.reshape(1, nseq, dec_seq, lat_w), kpe32.reshape(1, nseq, dec_seq, rope))
    return (y_prompt, y_sample) + new_p + new_s
```
